```python
import jax
import jax.numpy as jnp
from jax import lax
import numpy as np

D_MODEL = 2048
BATCH = 2
SEQ = 16384
DEPTH = 4

GRID_W = 64
CTX_LEN = 256
N_MIXERS = 4
N_MOD = 9
N_NORMS = 6
RMS_EPS = 1e-6
ROPE_BASE = 10000.0
Q_BLOCK = 128

D_FF = 5632
FFN_RES = 0.5

MLA_HEADS = D_MODEL // 128
MLA_Q_RANK = D_MODEL // 4
MLA_KV_RANK = D_MODEL // 4
MLA_NOPE = 128
MLA_ROPE = 64
MLA_V = 128

GQA_HEADS = D_MODEL // 128
GQA_KV_HEADS = GQA_HEADS // 4
GQA_HEAD_DIM = 128

NA_HEADS = D_MODEL // 128
NA_HEAD_DIM = 128
NA_WIN_R = 8
NA_WIN_C = 16

RET_HEADS = D_MODEL // 256
RET_QK_DIM = 256
RET_V_DIM = 512
RET_CHUNK = 128

kernel_name = 'hybrid_interleaved_diffusion_block'

f32 = jnp.float32


def rmsnorm(x, gain=None):
    xf = x.astype(f32)
    y = xf * lax.rsqrt(jnp.mean(xf * xf, axis=-1, keepdims=True) + RMS_EPS)
    if gain is not None:
        y = y * gain.astype(f32)
    return y.astype(x.dtype)


def modulate(u, shift, scale):
    return u * (1.0 + scale) + shift


def rope_angles_axial(n_tokens, rot_dim):
    t = jnp.arange(n_tokens, dtype=jnp.int32)
    row = (t // GRID_W).astype(f32)
    col = (t % GRID_W).astype(f32)
    n_f = rot_dim // 4
    freqs = ROPE_BASE ** (-jnp.arange(n_f, dtype=f32) / n_f)
    return jnp.concatenate([row[:, None] * freqs, col[:, None] * freqs], axis=-1)


def rope_angles_1d(n_tokens, rot_dim):
    pos = jnp.arange(n_tokens, dtype=f32)
    n_f = rot_dim // 2
    freqs = ROPE_BASE ** (-jnp.arange(n_f, dtype=f32) / n_f)
    return pos[:, None] * freqs


def apply_rope(x, ang):
    half = x.shape[-1] // 2
    cos = jnp.cos(ang).astype(x.dtype)
    sin = jnp.sin(ang).astype(x.dtype)
    x1, x2 = x[..., :half], x[..., half:]
    return jnp.concatenate([x1 * cos - x2 * sin, x2 * cos + x1 * sin], axis=-1)


def sweep_query_blocks(fn, q):
    B, S = q.shape[0], q.shape[1]
    nb = S // Q_BLOCK
    qb = jnp.moveaxis(q.reshape((B, nb, Q_BLOCK) + q.shape[2:]), 1, 0)
    out = lax.map(fn, qb)
    return jnp.moveaxis(out, 0, 1).reshape(B, S, out.shape[-1])


def ffn_half_step(h, shift, scale, gate, g_pre, g_post, w_gate, w_up, w_down):
    u = modulate(rmsnorm(h, g_pre), shift, scale)
    y = (jax.nn.silu(u @ w_gate) * (u @ w_up)) @ w_down
    return h + FFN_RES * gate * rmsnorm(y, g_post)


def mla_mixer(u_lat, u_ctx, w_in, q_gain, kv_gain, w_uq, w_ukv, w_o, ctx_out):
    H = MLA_HEADS
    scale = (MLA_NOPE + MLA_ROPE) ** -0.5

    def proj(u):
        Bu, T, _ = u.shape
        c_q, c_kv, k_rope = jnp.split(u @ w_in, [MLA_Q_RANK, MLA_Q_RANK + MLA_KV_RANK], axis=-1)
        q = (rmsnorm(c_q, q_gain) @ w_uq).reshape(Bu, T, H, MLA_NOPE + MLA_ROPE)
        kv = (rmsnorm(c_kv, kv_gain) @ w_ukv).reshape(Bu, T, H, MLA_NOPE + MLA_V)
        return q, kv[..., :MLA_NOPE], kv[..., MLA_NOPE:], k_rope

    ql, knl, vl, krl = proj(u_lat)
    qc, knc, vc, krc = proj(u_ctx)
    ang = rope_angles_axial(u_lat.shape[1], MLA_ROPE)
    ql = jnp.concatenate([ql[..., :MLA_NOPE], apply_rope(ql[..., MLA_NOPE:], ang[:, None, :])], axis=-1)
    krl = apply_rope(krl, ang)
    kn_all = jnp.concatenate([knc, knl], axis=1)
    kr_all = jnp.concatenate([krc, krl], axis=1)
    v_all = jnp.concatenate([vc, vl], axis=1)

    def attend(q, k_nope, k_rope, v):
        Bq, Q = q.shape[0], q.shape[1]
        s = (jnp.einsum('bqhd,bthd->bhqt', q[..., :MLA_NOPE], k_nope)
             + jnp.einsum('bqhr,btr->bhqt', q[..., MLA_NOPE:], k_rope)).astype(f32) * scale
        p = jax.nn.softmax(s, axis=-1).astype(v.dtype)
        return jnp.einsum('bhqt,bthd->bqhd', p, v).reshape(Bq, Q, H * MLA_V)

    y_lat = sweep_query_blocks(lambda qb: attend(qb, kn_all, kr_all, v_all), ql) @ w_o
    y_ctx = attend(qc, knc, krc, vc) @ w_o if ctx_out else None
    return y_lat, y_ctx


def gqa_mixer(u_lat, u_ctx, w_in, q_gain, k_gain, w_o, ctx_out):
    H, Hk, hd = GQA_HEADS, GQA_KV_HEADS, GQA_HEAD_DIM
    G = H // Hk
    scale = hd ** -0.5

    def proj(u):
        Bu, T, _ = u.shape
        q, k, v = jnp.split(u @ w_in, [H * hd, (H + Hk) * hd], axis=-1)
        return (rmsnorm(q.reshape(Bu, T, H, hd), q_gain),
                rmsnorm(k.reshape(Bu, T, Hk, hd), k_gain),
                v.reshape(Bu, T, Hk, hd))

    ql, kl, vl = proj(u_lat)
    qc, kc, vc = proj(u_ctx)
    ang = rope_angles_axial(u_lat.shape[1], hd)[:, None, :]
    ql = apply_rope(ql, ang)
    kl = apply_rope(kl, ang)
    k_all = jnp.concatenate([kc, kl], axis=1)
    v_all = jnp.concatenate([vc, vl], axis=1)

    def attend(q, k, v):
        Bq, Q = q.shape[0], q.shape[1]
        s = jnp.einsum('bqkgd,btkd->bkgqt', q.reshape(Bq, Q, Hk, G, hd), k).astype(f32) * scale
        p = jax.nn.softmax(s, axis=-1).astype(v.dtype)
        return jnp.einsum('bkgqt,btkd->bqkgd', p, v).reshape(Bq, Q, H * hd)

    y_lat = sweep_query_blocks(lambda qb: attend(qb, k_all, v_all), ql) @ w_o
    y_ctx = attend(qc, kc, vc) @ w_o if ctx_out else None
    return y_lat, y_ctx


def na_mixer(u_lat, u_ctx, w_in, rpb, w_o, ctx_out):
    H, hd = NA_HEADS, NA_HEAD_DIM
    B, S, _ = u_lat.shape
    rows = S // GRID_W
    wr, wc = min(NA_WIN_R, rows), NA_WIN_C
    scale = hd ** -0.5

    def proj(u):
        Bu, T, _ = u.shape
        q, k, v = jnp.split(u @ w_in, 3, axis=-1)
        return q.reshape(Bu, T, H, hd), k.reshape(Bu, T, H, hd), v.reshape(Bu, T, H, hd)

    ql, kl, vl = proj(u_lat)
    qc, kc, vc = proj(u_ctx)
    k_grid = kl.reshape(B, rows, GRID_W, H, hd)
    v_grid = vl.reshape(B, rows, GRID_W, H, hd)
    col = jnp.arange(GRID_W, dtype=jnp.int32)
    col_start = jnp.clip(col - wc // 2, 0, GRID_W - wc)
    col_idx = col_start[:, None] + jnp.arange(wc, dtype=jnp.int32)[None, :]
    col_bias = col_idx - col[:, None] + (NA_WIN_C - 1)

    def row_block(args):
        q_row, r = args
        r0 = jnp.clip(r - wr // 2, 0, rows - wr)
        k_band = lax.dynamic_slice_in_dim(k_grid, r0, wr, axis=1)
        v_band = lax.dynamic_slice_in_dim(v_grid, r0, wr, axis=1)
        k_nb = k_band[:, :, col_idx]
        v_nb = v_band[:, :, col_idx]
        row_bias = r0 + jnp.arange(wr, dtype=jnp.int32) - r + (NA_WIN_R - 1)
        bias = rpb[:, row_bias[None, :, None], col_bias[:, None, :]]
        s_nb = jnp.einsum('bqhd,bwqkhd->bhqwk', q_row, k_nb).astype(f32) * scale + bias.astype(f32)
        s_cx = jnp.einsum('bqhd,blhd->bhql', q_row, kc).astype(f32) * scale
        s = jnp.concatenate([s_nb.reshape(B, H, GRID_W, wr * wc), s_cx], axis=-1)
        p = jax.nn.softmax(s, axis=-1).astype(v_nb.dtype)
        p_nb = p[..., :wr * wc].reshape(B, H, GRID_W, wr, wc)
        o = (jnp.einsum('bhqwk,bwqkhd->bqhd', p_nb, v_nb)
             + jnp.einsum('bhql,blhd->bqhd', p[..., wr * wc:], vc))
        return o.reshape(B, GRID_W, H * hd)

    q_rows = jnp.moveaxis(ql.reshape(B, rows, GRID_W, H, hd), 1, 0)
    o = lax.map(row_block, (q_rows, jnp.arange(rows, dtype=jnp.int32)))
    y_lat = jnp.moveaxis(o, 0, 1).reshape(B, S, H * hd) @ w_o
    y_ctx = None
    if ctx_out:
        s = jnp.einsum('bqhd,blhd->bhql', qc, kc).astype(f32) * scale
        p = jax.nn.softmax(s, axis=-1).astype(vc.dtype)
        y_ctx = jnp.einsum('bhql,blhd->bqhd', p, vc).reshape(qc.shape[0], qc.shape[1], H * hd) @ w_o
    return y_lat, y_ctx


def retention_chunks(q, k, v, log_decay, state0):
    B, T, H, _ = q.shape
    dv = v.shape[-1]
    C = RET_CHUNK
    n = T // C
    pos = jnp.arange(C, dtype=f32)
    rel = pos[:, None] - pos[None, :]
    d_in = jnp.where(rel >= 0, jnp.exp(jnp.maximum(rel, 0.0)[None] * log_decay[:, None, None]), 0.0)
    xi = jnp.exp((pos + 1.0)[:, None] * log_decay[None, :])
    zeta = jnp.exp((C - 1.0 - pos)[:, None] * log_decay[None, :])
    g_c = jnp.exp(C * log_decay)

    def to_chunks(a):
        return jnp.moveaxis(a.reshape(B, n, C, H, a.shape[-1]), 1, 0)

    def step(state, blk):
        qb, kb, vb = blk
        s = jnp.einsum('bihd,bjhd->bhij', qb, kb) * d_in
        o = (jnp.einsum('bhij,bjhe->bihe', s, vb)
             + jnp.einsum('bihd,bhde->bihe', qb, state) * xi[None, :, :, None])
        state = state * g_c[None, :, None, None] + jnp.einsum('bjhd,bjhe->bhde', kb * zeta[None, :, :, None], vb)
        return state, o

    _, o = lax.scan(step, state0, (to_chunks(q), to_chunks(k), to_chunks(v)))
    return jnp.moveaxis(o, 0, 1).reshape(B, T, H, dv)


def retention_final_state(k, v, log_decay):
    T = k.shape[1]
    w = jnp.exp((T - 1.0 - jnp.arange(T, dtype=f32))[:, None] * log_decay[None, :])
    return jnp.einsum('bthd,bthe->bhde', k * w[None, :, :, None], v)


def retention_mixer(u_lat, u_ctx, w_in, decay_log2, w_o, ctx_out):
    H, dk, dv = RET_HEADS, RET_QK_DIM, RET_V_DIM
    lg = jnp.log1p(-jnp.exp2(decay_log2.astype(f32)))
    lg_f, lg_b = lg[0], lg[1]

    def proj(u):
        Bu, T, _ = u.shape
        z = (u @ w_in).astype(f32)
        q, k, v, g = jnp.split(z, [H * dk, 2 * H * dk, 2 * H * dk + H * dv], axis=-1)
        return (q.reshape(Bu, T, H, dk), k.reshape(Bu, T, H, dk) * dk ** -0.5,
                v.reshape(Bu, T, H, dv), g)

    ql, kl, vl, gl = proj(u_lat)
    qc, kc, vc, gc = proj(u_ctx)
    ang = rope_angles_1d(u_lat.shape[1], dk)[:, None, :]
    ql = apply_rope(ql, ang)
    kl = apply_rope(kl, ang)

    def flip(a):
        return jnp.flip(a, axis=1)

    s_f = retention_final_state(kc, vc, lg_f)
    s_b = retention_final_state(flip(kc), flip(vc), lg_b)
    o_lat = (retention_chunks(ql, kl, vl, lg_f, s_f)
             + flip(retention_chunks(flip(ql), flip(kl), flip(vl), lg_b, s_b)))

    def out(o, g):
        Bo, T = o.shape[0], o.shape[1]
        y = rmsnorm(o).reshape(Bo, T, H * dv)
        return ((jax.nn.silu(g) * y) @ w_o).astype(u_lat.dtype)

    y_lat = out(o_lat, gl)
    y_ctx = None
    if ctx_out:
        zero = jnp.zeros_like(s_f)
        o_ctx = (retention_chunks(qc, kc, vc, lg_f, zero)
                 + flip(retention_chunks(flip(qc), flip(kc), flip(vc), lg_b, zero)))
        y_ctx = out(o_ctx, gc)
    return y_lat, y_ctx


def setup_inputs(seed: int = 0) -> dict:
    key = jax.random.key(seed)
    keys = iter(jax.random.split(key, 32))
    D = D_MODEL
    n_a, n_b, n_c, n_d = [len(range(t, DEPTH, N_MIXERS)) for t in range(N_MIXERS)]

    def normal(shape, std):
        return std * jax.random.normal(next(keys), shape, f32)

    def dense(shape):
        return normal(shape, shape[-2] ** -0.5)

    def gain(shape):
        return 1.0 + normal(shape, 0.05)

    decay0 = -5.0 - jnp.arange(RET_HEADS, dtype=f32)
    return {
        'x': normal((BATCH, SEQ, D), 1.0),
        'c': normal((BATCH, D), 1.0),
        'ctx': normal((BATCH, CTX_LEN, D), 1.0),
        'c_ctx': normal((D,), 1.0),
        'mod_w': dense((DEPTH, D, N_MOD * D)),
        'mod_b': normal((DEPTH, N_MOD * D), 0.02),
        'norm_g': gain((DEPTH, N_NORMS, D)),
        'ffn_w_gate': dense((DEPTH, 2, D, D_FF)),
        'ffn_w_up': dense((DEPTH, 2, D, D_FF)),
        'ffn_w_down': dense((DEPTH, 2, D_FF, D)),
        'mla_w_in': dense((n_a, D, MLA_Q_RANK + MLA_KV_RANK + MLA_ROPE)),
        'mla_q_gain': gain((n_a, MLA_Q_RANK)),
        'mla_kv_gain': gain((n_a, MLA_KV_RANK)),
        'mla_w_uq': dense((n_a, MLA_Q_RANK, MLA_HEADS * (MLA_NOPE + MLA_ROPE))),
        'mla_w_ukv': dense((n_a, MLA_KV_RANK, MLA_HEADS * (MLA_NOPE + MLA_V))),
        'mla_w_o': dense((n_a, MLA_HEADS * MLA_V, D)),
        'gqa_w_in': dense((n_b, D, (GQA_HEADS + 2 * GQA_KV_HEADS) * GQA_HEAD_DIM)),
        'gqa_q_gain': gain((n_b, GQA_HEAD_DIM)),
        'gqa_k_gain': gain((n_b, GQA_HEAD_DIM)),
        'gqa_w_o': dense((n_b, GQA_HEADS * GQA_HEAD_DIM, D)),
        'na_w_in': dense((n_c, D, 3 * NA_HEADS * NA_HEAD_DIM)),
        'na_rpb': normal((n_c, NA_HEADS, 2 * NA_WIN_R - 1, 2 * NA_WIN_C - 1), 0.2),
        'na_w_o': dense((n_c, NA_HEADS * NA_HEAD_DIM, D)),
        'ret_w_in': dense((n_d, D, 2 * RET_HEADS * RET_QK_DIM + 2 * RET_HEADS * RET_V_DIM)),
        'ret_decay_log2': decay0 + normal((n_d, 2, RET_HEADS), 0.1),
        'ret_w_o': dense((n_d, RET_HEADS * RET_V_DIM, D)),
    }


def reference(x, c, ctx, c_ctx, mod_w, mod_b, norm_g, ffn_w_gate, ffn_w_up, ffn_w_down,
              mla_w_in, mla_q_gain, mla_kv_gain, mla_w_uq, mla_w_ukv, mla_w_o,
              gqa_w_in, gqa_q_gain, gqa_k_gain, gqa_w_o,
              na_w_in, na_rpb, na_w_o,
              ret_w_in, ret_decay_log2, ret_w_o):
    B, _, D = x.shape
    h, hc = x, ctx
    s_lat = jax.nn.silu(c)
    s_ctx = jax.nn.silu(c_ctx)[None]
    for i in range(DEPTH):
        kind, j = i % N_MIXERS, i // N_MIXERS
        last = i == DEPTH - 1
        m_l = (s_lat @ mod_w[i] + mod_b[i]).reshape(B, 1, N_MOD, D)
        m_c = (s_ctx @ mod_w[i] + mod_b[i]).reshape(1, 1, N_MOD, D)
        gn = norm_g[i]

        def ffn(hh, m, f):
            return ffn_half_step(hh, m[:, :, 3 * f], m[:, :, 3 * f + 1], m[:, :, 3 * f + 2],
                                 gn[2 * f], gn[2 * f + 1],
                                 ffn_w_gate[i, f // 2], ffn_w_up[i, f // 2], ffn_w_down[i, f // 2])

        h = ffn(h, m_l, 0)
        hc = ffn(hc, m_c, 0)
        u = modulate(rmsnorm(h, gn[2]), m_l[:, :, 3], m_l[:, :, 4])
        uc = modulate(rmsnorm(hc, gn[2]), m_c[:, :, 3], m_c[:, :, 4])
        if kind == 0:
            y, yc = mla_mixer(u, uc, mla_w_in[j], mla_q_gain[j], mla_kv_gain[j],
                              mla_w_uq[j], mla_w_ukv[j], mla_w_o[j], not last)
        elif kind == 1:
            y, yc = gqa_mixer(u, uc, gqa_w_in[j], gqa_q_gain[j], gqa_k_gain[j], gqa_w_o[j], not last)
        elif kind == 2:
            y, yc = na_mixer(u, uc, na_w_in[j], na_rpb[j], na_w_o[j], not last)
        else:
            y, yc = retention_mixer(u, uc, ret_w_in[j], ret_decay_log2[j], ret_w_o[j], not last)
        h = h + m_l[:, :, 5] * rmsnorm(y, gn[3])
        h = ffn(h, m_l, 2)
        if not last:
            hc = hc + m_c[:, :, 5] * rmsnorm(yc, gn[3])
            hc = ffn(hc, m_c, 2)
    return h
```

```python
import functools

import jax
import jax.numpy as jnp
from jax import lax
from jax.experimental import pallas as pl
from jax.experimental.pallas import tpu as pltpu

F32 = jnp.float32
BF16 = jnp.bfloat16

RMS_EPS = 1e-6
ROPE_BASE = 10000.0
FFN_RES = 0.5
GRID_W = 64
N_MOD = 9
LANES = 128
HEAD_DIM = 128
MLA_ROPE = 64
RET_QK = 256
RET_V = 512
RET_CHUNK = 128
NA_WIN_R = 8
NA_WIN_C = 16
NA_QROWS = 8
NEG_INF = -1e30
VMEM_LIMIT_BYTES = 56 * 1024 * 1024


def _cparams(semantics):
    return pltpu.CompilerParams(dimension_semantics=semantics, vmem_limit_bytes=VMEM_LIMIT_BYTES)


def _tile(n, pref):
    if n <= pref:
        return n
    t = pref
    while n % t:
        t //= 2
    return t


def _rms(x):
    return x * lax.rsqrt(jnp.mean(x * x, axis=-1, keepdims=True) + RMS_EPS)


def _silu(x):
    return x * jax.nn.sigmoid(x)


def _mod_kernel(cc_ref, w_ref, b_ref, o_ref):
    s = _silu(cc_ref[...]).astype(BF16)
    o_ref[0] = jnp.dot(s, w_ref[0].astype(BF16), preferred_element_type=F32) + b_ref[0]


def _mod_vectors(cc, mod_w, mod_b):
    depth, d, n = mod_w.shape
    rows = cc.shape[0]
    tn = _tile(n, 1024)
    return pl.pallas_call(
        _mod_kernel,
        grid=(depth, n // tn),
        in_specs=[
            pl.BlockSpec((rows, d), lambda l, j: (0, 0)),
            pl.BlockSpec((1, d, tn), lambda l, j: (l, 0, j)),
            pl.BlockSpec((1, 1, tn), lambda l, j: (l, 0, j)),
        ],
        out_specs=pl.BlockSpec((1, rows, tn), lambda l, j: (l, 0, j)),
        out_shape=jax.ShapeDtypeStruct((depth, rows, n), F32),
        compiler_params=_cparams(("arbitrary", "arbitrary")),
        name="mod_vectors",
    )(cc, mod_w, mod_b.reshape(depth, 1, n))


def _ffn_kernel(h_ref, sh_ref, sc_ref, gt_ref, gpre_ref, gpost_ref, wg_ref, wu_ref, wd_ref,
                o_ref, u_ref, acc_ref, *, nk):
    k = pl.program_id(1)

    @pl.when(k == 0)
    def _():
        u = _rms(h_ref[...]) * gpre_ref[...]
        u_ref[...] = (u * (1.0 + sc_ref[0]) + sh_ref[0]).astype(BF16)
        acc_ref[...] = jnp.zeros_like(acc_ref)

    u = u_ref[...]
    g = jnp.dot(u, wg_ref[...], preferred_element_type=F32)
    up = jnp.dot(u, wu_ref[...], preferred_element_type=F32)
    a = (_silu(g) * up).astype(BF16)
    acc_ref[...] += jnp.dot(a, wd_ref[...], preferred_element_type=F32)

    @pl.when(k == nk - 1)
    def _():
        y = _rms(acc_ref[...]) * gpost_ref[...]
        o_ref[...] = h_ref[...] + FFN_RES * gt_ref[0] * y


def _ffn(h, shift, scale, gate, g_pre, g_post, wg, wu, wd, *, tm=512, tf=512):
    m, d = h.shape
    f = wg.shape[1]
    bm = shift.shape[0]
    rows_per_b = m // bm
    tm = _tile(rows_per_b, tm)
    tf = _tile(f, tf)
    nk = f // tf
    per_b = rows_per_b // tm
    mod_spec = pl.BlockSpec((1, 1, d), lambda i, k: (i // per_b, 0, 0))
    vec_spec = pl.BlockSpec((1, d), lambda i, k: (0, 0))
    return pl.pallas_call(
        functools.partial(_ffn_kernel, nk=nk),
        grid=(m // tm, nk),
        in_specs=[
            pl.BlockSpec((tm, d), lambda i, k: (i, 0)),
            mod_spec, mod_spec, mod_spec, vec_spec, vec_spec,
            pl.BlockSpec((d, tf), lambda i, k: (0, k)),
            pl.BlockSpec((d, tf), lambda i, k: (0, k)),
            pl.BlockSpec((tf, d), lambda i, k: (k, 0)),
        ],
        out_specs=pl.BlockSpec((tm, d), lambda i, k: (i, 0)),
        out_shape=jax.ShapeDtypeStruct((m, d), F32),
        scratch_shapes=[pltpu.VMEM((tm, d), BF16), pltpu.VMEM((tm, d), F32)],
        compiler_params=_cparams(("parallel", "arbitrary")),
        name="ffn_half_step",
    )(h, shift, scale, gate, g_pre, g_post, wg, wu, wd)


def _proj_kernel(*refs, modulated):
    if modulated:
        x_ref, g_ref, sh_ref, sc_ref, w_ref, o_ref, u_ref = refs
    else:
        x_ref, g_ref, w_ref, o_ref, u_ref = refs

    @pl.when(pl.program_id(1) == 0)
    def _():
        u = _rms(x_ref[...].astype(F32)) * g_ref[...]
        if modulated:
            u = u * (1.0 + sc_ref[0]) + sh_ref[0]
        u_ref[...] = u.astype(BF16)

    o_ref[...] = jnp.dot(u_ref[...], w_ref[...], preferred_element_type=F32).astype(o_ref.dtype)


def _proj(x, gain, w, *, out_dtype, shift=None, scale=None, kblock=0, tm=512):
    m = x.shape[0]
    k, n = w.shape
    modulated = shift is not None
    bm = shift.shape[0] if modulated else 1
    rows_per_b = m // bm
    tm = _tile(rows_per_b, tm)
    per_b = rows_per_b // tm
    tn = n if k * n * 2 <= 6 * 1024 * 1024 else _tile(n, 1024)
    in_specs = [pl.BlockSpec((tm, k), lambda i, j: (i, kblock)),
                pl.BlockSpec((1, k), lambda i, j: (0, 0))]
    args = [x, gain]
    if modulated:
        mod_spec = pl.BlockSpec((1, 1, k), lambda i, j: (i // per_b, 0, 0))
        in_specs += [mod_spec, mod_spec]
        args += [shift, scale]
    in_specs.append(pl.BlockSpec((k, tn), lambda i, j: (0, j)))
    args.append(w)
    return pl.pallas_call(
        functools.partial(_proj_kernel, modulated=modulated),
        grid=(m // tm, n // tn),
        in_specs=in_specs,
        out_specs=pl.BlockSpec((tm, tn), lambda i, j: (i, j)),
        out_shape=jax.ShapeDtypeStruct((m, n), out_dtype),
        scratch_shapes=[pltpu.VMEM((tm, k), BF16)],
        compiler_params=_cparams(("parallel", "arbitrary")),
        name="norm_proj",
    )(*args)


def _out_proj_kernel(a_ref, w_ref, gpost_ref, gt_ref, h_ref, o_ref, acc_ref, *, nk):
    k = pl.program_id(1)

    @pl.when(k == 0)
    def _():
        acc_ref[...] = jnp.zeros_like(acc_ref)

    acc_ref[...] += jnp.dot(a_ref[...], w_ref[...], preferred_element_type=F32)

    @pl.when(k == nk - 1)
    def _():
        o_ref[...] = h_ref[...] + gt_ref[0] * (_rms(acc_ref[...]) * gpost_ref[...])


def _out_proj(a, w, g_post, gate, h, *, tm=512, tk=1024):
    m, kdim = a.shape
    d = w.shape[1]
    bm = gate.shape[0]
    rows_per_b = m // bm
    tm = _tile(rows_per_b, tm)
    per_b = rows_per_b // tm
    tk = _tile(kdim, tk)
    nk = kdim // tk
    return pl.pallas_call(
        functools.partial(_out_proj_kernel, nk=nk),
        grid=(m // tm, nk),
        in_specs=[
            pl.BlockSpec((tm, tk), lambda i, k: (i, k)),
            pl.BlockSpec((tk, d), lambda i, k: (k, 0)),
            pl.BlockSpec((1, d), lambda i, k: (0, 0)),
            pl.BlockSpec((1, 1, d), lambda i, k: (i // per_b, 0, 0)),
            pl.BlockSpec((tm, d), lambda i, k: (i, 0)),
        ],
        out_specs=pl.BlockSpec((tm, d), lambda i, k: (i, 0)),
        out_shape=jax.ShapeDtypeStruct((m, d), F32),
        scratch_shapes=[pltpu.VMEM((tm, d), F32)],
        compiler_params=_cparams(("parallel", "arbitrary")),
        name="out_proj",
    )(a, w, g_post, gate, h)


def _rope128(x, c_ref, s_ref):
    return x * c_ref[...] + pltpu.roll(x, 64, axis=1) * s_ref[...]


def _prep_kernel(*refs, mode, rope, norm, scale):
    refs = list(refs)
    o_ref = refs.pop()
    x_ref = refs.pop(0)
    x2_ref = refs.pop(0) if mode == "mla_k" else None
    g_ref = refs.pop(0) if norm else None
    c_ref, s_ref = (refs.pop(0), refs.pop(0)) if rope else (None, None)

    x = x_ref[0].astype(F32)
    if mode == "head":
        if norm:
            x = _rms(x) * g_ref[...]
        if rope:
            x = _rope128(x, c_ref, s_ref)
        o_ref[0] = (x * scale).astype(o_ref.dtype)
    elif mode == "mla_q":
        r = x[:, HEAD_DIM:]
        if rope:
            r = _rope128(r, c_ref, s_ref)
        o_ref[0, :, :HEAD_DIM] = (x[:, :HEAD_DIM] * scale).astype(o_ref.dtype)
        o_ref[0, :, HEAD_DIM:] = (r * scale).astype(o_ref.dtype)
    else:
        r = x2_ref[0].astype(F32)
        if rope:
            r = _rope128(r, c_ref, s_ref)
        o_ref[0, :, :HEAD_DIM] = x.astype(o_ref.dtype)
        o_ref[0, :, HEAD_DIM:] = r.astype(o_ref.dtype)


def _prep(x, *, mode, heads, xcol, x2=None, x2col=None, gain=None, tables=None, scale=1.0, ts=512):
    b, t, _ = x.shape
    ts = _tile(t, ts)
    wx = HEAD_DIM if mode in ("head", "mla_k") else 2 * HEAD_DIM
    wo = HEAD_DIM if mode == "head" else 2 * HEAD_DIM
    in_specs = [pl.BlockSpec((1, ts, wx), lambda bb, i, h: (bb, i, xcol(h)))]
    args = [x]
    if mode == "mla_k":
        in_specs.append(pl.BlockSpec((1, ts, LANES), lambda bb, i, h: (bb, i, x2col)))
        args.append(x2)
    if gain is not None:
        in_specs.append(pl.BlockSpec((1, HEAD_DIM), lambda bb, i, h: (0, 0)))
        args.append(gain)
    if tables is not None:
        tspec = pl.BlockSpec((ts, LANES), lambda bb, i, h: (i, 0))
        in_specs += [tspec, tspec]
        args += list(tables)
    return pl.pallas_call(
        functools.partial(_prep_kernel, mode=mode, rope=tables is not None, norm=gain is not None,
                          scale=scale),
        grid=(b, t // ts, heads),
        in_specs=in_specs,
        out_specs=pl.BlockSpec((1, ts, wo), lambda bb, i, h: (bb, i, h)),
        out_shape=jax.ShapeDtypeStruct((b, t, heads * wo), BF16),
        compiler_params=_cparams(("parallel", "parallel", "arbitrary")),
        name="head_prep_" + mode,
    )(*args)


def _flash_kernel(*refs, nk, scale, has_lat):
    if has_lat:
        q_ref, kc_ref, vc_ref, kl_ref, vl_ref, o_ref, m_ref, l_ref, acc_ref = refs
    else:
        q_ref, kc_ref, vc_ref, o_ref, m_ref, l_ref, acc_ref = refs
    kk = pl.program_id(3)

    def step(k, v):
        q = q_ref[0]
        if scale != 1.0:
            q = (q.astype(F32) * scale).astype(BF16)
        s = lax.dot_general(q, k, (((1,), (1,)), ((), ())), preferred_element_type=F32)
        m_prev = m_ref[...]
        m_new = jnp.maximum(m_prev, jnp.max(s, axis=-1, keepdims=True))
        alpha = jnp.exp(m_prev - m_new)
        p = jnp.exp(s - m_new)
        l_ref[...] = alpha * l_ref[...] + jnp.sum(p, axis=-1, keepdims=True)
        acc_ref[...] = alpha * acc_ref[...] + jnp.dot(p.astype(BF16), v, preferred_element_type=F32)
        m_ref[...] = m_new

    @pl.when(kk == 0)
    def _():
        m_ref[...] = jnp.full_like(m_ref, NEG_INF)
        l_ref[...] = jnp.zeros_like(l_ref)
        acc_ref[...] = jnp.zeros_like(acc_ref)
        step(kc_ref[0], vc_ref[0])

    if has_lat:
        @pl.when(kk > 0)
        def _():
            step(kl_ref[0], vl_ref[0])

    @pl.when(kk == nk - 1)
    def _():
        o_ref[0] = (acc_ref[...] / l_ref[...]).astype(o_ref.dtype)


def _flash(q, kc, vc, kl=None, vl=None, *, heads, dq, dv, qcol, kcol, vcol, scale=1.0,
           tq=1024, tk=1024):
    b, s, _ = q.shape
    tc = kc.shape[1]
    tq = _tile(s, tq)
    has_lat = kl is not None
    if has_lat:
        tk = _tile(kl.shape[1], tk)
        nk = 1 + kl.shape[1] // tk
    else:
        nk = 1
    in_specs = [
        pl.BlockSpec((1, tq, dq), lambda bb, h, i, kk: (bb, i, qcol(h))),
        pl.BlockSpec((1, tc, dq), lambda bb, h, i, kk: (bb, 0, kcol(h))),
        pl.BlockSpec((1, tc, dv), lambda bb, h, i, kk: (bb, 0, vcol(h))),
    ]
    args = [q, kc, vc]
    if has_lat:
        in_specs += [
            pl.BlockSpec((1, tk, dq), lambda bb, h, i, kk: (bb, jnp.maximum(kk - 1, 0), kcol(h))),
            pl.BlockSpec((1, tk, dv), lambda bb, h, i, kk: (bb, jnp.maximum(kk - 1, 0), vcol(h))),
        ]
        args += [kl, vl]
    return pl.pallas_call(
        functools.partial(_flash_kernel, nk=nk, scale=scale, has_lat=has_lat),
        grid=(b, heads, s // tq, nk),
        in_specs=in_specs,
        out_specs=pl.BlockSpec((1, tq, dv), lambda bb, h, i, kk: (bb, i, h)),
        out_shape=jax.ShapeDtypeStruct((b, s, heads * dv), BF16),
        scratch_shapes=[pltpu.VMEM((tq, 1), F32), pltpu.VMEM((tq, 1), F32), pltpu.VMEM((tq, dv), F32)],
        compiler_params=_cparams(("parallel", "parallel", "parallel", "arbitrary")),
        name="flash_attention",
    )(*args)


def _na_kernel(q_ref, k_ref, v_ref, kc_ref, vc_ref, bias_ref, o_ref, *, band, seq, scale):
    i = pl.program_id(2)
    tq = q_ref.shape[1]
    start = jnp.clip(i * tq - (band - tq) // 2, 0, seq - band)
    start = pl.multiple_of(start, 256)
    q = q_ref[0]
    kb = k_ref[0, pl.ds(start, band), :]
    vb = v_ref[0, pl.ds(start, band), :]
    dn = (((1,), (1,)), ((), ()))
    s_nb = lax.dot_general(q, kb, dn, preferred_element_type=F32) * scale + bias_ref[0, 0]
    s_cx = lax.dot_general(q, kc_ref[0], dn, preferred_element_type=F32) * scale
    m = jnp.maximum(jnp.max(s_nb, axis=-1, keepdims=True), jnp.max(s_cx, axis=-1, keepdims=True))
    p_nb = jnp.exp(s_nb - m)
    p_cx = jnp.exp(s_cx - m)
    l = jnp.sum(p_nb, axis=-1, keepdims=True) + jnp.sum(p_cx, axis=-1, keepdims=True)
    o = (jnp.dot(p_nb.astype(BF16), vb, preferred_element_type=F32)
         + jnp.dot(p_cx.astype(BF16), vc_ref[0], preferred_element_type=F32))
    o_ref[0] = (o / l).astype(o_ref.dtype)


def _na_bias_tiles(rpb, rows):
    nr, wr, wc = NA_QROWS, NA_WIN_R, NA_WIN_C
    brows = nr + wr
    a = jnp.arange(nr, dtype=jnp.int32)[:, None, None, None]
    c = jnp.arange(GRID_W, dtype=jnp.int32)[None, :, None, None]
    bq = jnp.arange(brows, dtype=jnp.int32)[None, None, :, None]
    kc = jnp.arange(GRID_W, dtype=jnp.int32)[None, None, None, :]
    c0 = jnp.clip(c - wc // 2, 0, GRID_W - wc)
    in_col = (kc >= c0) & (kc < c0 + wc)
    tiles = []
    for r_first, b_first in ((0, 0), (nr, nr - wr // 2), (rows - nr, rows - brows)):
        r = r_first + a
        kr = b_first + bq
        r0 = jnp.clip(r - wr // 2, 0, rows - wr)
        ok = (kr >= r0) & (kr < r0 + wr) & in_col
        ri = jnp.clip(kr - r + (NA_WIN_R - 1), 0, 2 * NA_WIN_R - 2)
        ci = jnp.clip(kc - c + (NA_WIN_C - 1), 0, 2 * NA_WIN_C - 2)
        ri, ci = jnp.broadcast_arrays(ri, ci)
        vals = rpb[:, ri, ci].astype(F32)
        tile = jnp.where(ok[None], vals, NEG_INF)
        tiles.append(tile.reshape(rpb.shape[0], nr * GRID_W, brows * GRID_W))
    return jnp.stack(tiles, axis=1)


def _na_attention(z, zc, rpb, *, heads):
    b, s, _ = z.shape
    tcx = zc.shape[1]
    rows = s // GRID_W
    tq = NA_QROWS * GRID_W
    band = (NA_QROWS + NA_WIN_R) * GRID_W
    nblk = s // tq
    bias = _na_bias_tiles(rpb, rows)
    scale = HEAD_DIM ** -0.5

    def variant(i):
        return jnp.where(i == 0, 0, jnp.where(i == nblk - 1, 2, 1))

    return pl.pallas_call(
        functools.partial(_na_kernel, band=band, seq=s, scale=scale),
        grid=(b, heads, nblk),
        in_specs=[
            pl.BlockSpec((1, tq, HEAD_DIM), lambda bb, h, i: (bb, i, h)),
            pl.BlockSpec((1, s, HEAD_DIM), lambda bb, h, i: (bb, 0, heads + h)),
            pl.BlockSpec((1, s, HEAD_DIM), lambda bb, h, i: (bb, 0, 2 * heads + h)),
            pl.BlockSpec((1, tcx, HEAD_DIM), lambda bb, h, i: (bb, 0, heads + h)),
            pl.BlockSpec((1, tcx, HEAD_DIM), lambda bb, h, i: (bb, 0, 2 * heads + h)),
            pl.BlockSpec((1, 1, tq, band), lambda bb, h, i: (h, variant(i), 0, 0)),
        ],
        out_specs=pl.BlockSpec((1, tq, HEAD_DIM), lambda bb, h, i: (bb, i, h)),
        out_shape=jax.ShapeDtypeStruct((b, s, heads * HEAD_DIM), BF16),
        compiler_params=_cparams(("parallel", "parallel", "arbitrary")),
        name="neighbourhood_attention",
    )(z, z, z, zc, zc, bias)


def _ret_kernel(dec_ref, ql_ref, kl_ref, vl_ref, kc_ref, vc_ref, cos_ref, sin_ref, o_ref, st_ref,
                *, nc, chunk):
    h = pl.program_id(1)
    d = pl.program_id(2)
    t = pl.program_id(3)
    half = RET_QK // 2
    lg = jnp.log1p(-jnp.exp2(jnp.full((1, 1), dec_ref[d, h], F32)))
    pos = lax.broadcasted_iota(jnp.int32, (chunk, 1), 0).astype(F32)
    fwd = d == 0

    @pl.when(t == 0)
    def _():
        st_ref[...] = jnp.zeros_like(st_ref)

    def rope(x):
        c, s = cos_ref[...], sin_ref[...]
        x1, x2 = x[:, :half], x[:, half:]
        return jnp.concatenate([x1 * c - x2 * s, x2 * c + x1 * s], axis=-1)

    def update(k, v):
        zeta = jnp.exp(jnp.where(fwd, chunk - 1.0 - pos, pos) * lg)
        kz = (k * zeta).astype(BF16)
        kv = lax.dot_general(kz, v, (((0,), (0,)), ((), ())), preferred_element_type=F32)
        st_ref[...] = st_ref[...] * jnp.exp(chunk * lg) + kv

    @pl.when(t < nc)
    def _():
        update(kc_ref[0].astype(F32) * RET_QK ** -0.5, vc_ref[0])

    @pl.when(t >= nc)
    def _():
        q = rope(ql_ref[0].astype(F32))
        k = rope(kl_ref[0].astype(F32) * RET_QK ** -0.5)
        v = vl_ref[0]
        ri = lax.broadcasted_iota(jnp.int32, (chunk, chunk), 0)
        ci = lax.broadcasted_iota(jnp.int32, (chunk, chunk), 1)
        dist = jnp.where(fwd, ri - ci, ci - ri)
        d_in = jnp.where(dist >= 0, jnp.exp(jnp.maximum(dist, 0).astype(F32) * lg), 0.0)
        qb = q.astype(BF16)
        s = lax.dot_general(qb, k.astype(BF16), (((1,), (1,)), ((), ())),
                            preferred_element_type=F32) * d_in
        xi = jnp.exp(jnp.where(fwd, pos + 1.0, chunk - pos) * lg)
        o = (jnp.dot(s.astype(BF16), v, preferred_element_type=F32)
             + jnp.dot(qb, st_ref[...].astype(BF16), preferred_element_type=F32) * xi)
        o_ref[0, 0] = o
        update(k, v)


def _retention(z, zc, decay_log2, cos, sin, *, heads):
    b, s, _ = z.shape
    tcx = zc.shape[1]
    chunk = RET_CHUNK
    nc, nl = tcx // chunk, s // chunk
    kcol0 = heads
    vcol0 = (2 * heads * RET_QK) // RET_V

    def cidx(d, t):
        c = jnp.minimum(t, nc - 1)
        return jnp.where(d == 0, c, nc - 1 - c)

    def lidx(d, t):
        c = jnp.maximum(t - nc, 0)
        return jnp.where(d == 0, c, nl - 1 - c)

    return pl.pallas_call(
        functools.partial(_ret_kernel, nc=nc, chunk=chunk),
        grid=(b, heads, 2, nc + nl),
        in_specs=[
            pl.BlockSpec(memory_space=pltpu.SMEM),
            pl.BlockSpec((1, chunk, RET_QK), lambda bb, h, d, t: (bb, lidx(d, t), h)),
            pl.BlockSpec((1, chunk, RET_QK), lambda bb, h, d, t: (bb, lidx(d, t), kcol0 + h)),
            pl.BlockSpec((1, chunk, RET_V), lambda bb, h, d, t: (bb, lidx(d, t), vcol0 + h)),
            pl.BlockSpec((1, chunk, RET_QK), lambda bb, h, d, t: (bb, cidx(d, t), kcol0 + h)),
            pl.BlockSpec((1, chunk, RET_V), lambda bb, h, d, t: (bb, cidx(d, t), vcol0 + h)),
            pl.BlockSpec((chunk, RET_QK // 2), lambda bb, h, d, t: (lidx(d, t), 0)),
            pl.BlockSpec((chunk, RET_QK // 2), lambda bb, h, d, t: (lidx(d, t), 0)),
        ],
        out_specs=pl.BlockSpec((1, 1, chunk, RET_V), lambda bb, h, d, t: (d, bb, lidx(d, t), h)),
        out_shape=jax.ShapeDtypeStruct((2, b, s, heads * RET_V), F32),
        scratch_shapes=[pltpu.VMEM((RET_QK, RET_V), F32)],
        compiler_params=_cparams(("parallel", "parallel", "arbitrary", "arbitrary")),
        name="retention_scan",
    )(decay_log2, z, z, z, zc, zc, cos, sin)


def _ret_gate_kernel(of_ref, ob_ref, g_ref, y_ref):
    o = of_ref[0, 0] + ob_ref[0, 0]
    y_ref[0] = (_silu(g_ref[0].astype(F32)) * _rms(o)).astype(y_ref.dtype)


def _ret_gate(o2, z, *, heads, ts=512):
    _, b, s, _ = o2.shape
    ts = _tile(s, ts)
    gcol0 = (2 * heads * RET_QK + heads * RET_V) // RET_V
    return pl.pallas_call(
        _ret_gate_kernel,
        grid=(b, s // ts, heads),
        in_specs=[
            pl.BlockSpec((1, 1, ts, RET_V), lambda bb, i, h: (0, bb, i, h)),
            pl.BlockSpec((1, 1, ts, RET_V), lambda bb, i, h: (1, bb, i, h)),
            pl.BlockSpec((1, ts, RET_V), lambda bb, i, h: (bb, i, gcol0 + h)),
        ],
        out_specs=pl.BlockSpec((1, ts, RET_V), lambda bb, i, h: (bb, i, h)),
        out_shape=jax.ShapeDtypeStruct((b, s, heads * RET_V), BF16),
        compiler_params=_cparams(("parallel", "parallel", "arbitrary")),
        name="retention_gate",
    )(o2, o2, z)


def _axial_angles(n_tokens, rot_dim):
    t = jnp.arange(n_tokens, dtype=jnp.int32)
    row = (t // GRID_W).astype(F32)
    col = (t % GRID_W).astype(F32)
    n_f = rot_dim // 4
    freqs = ROPE_BASE ** (-jnp.arange(n_f, dtype=F32) / n_f)
    return jnp.concatenate([row[:, None] * freqs, col[:, None] * freqs], axis=-1)


def _rope_slot_tables(ang):
    pad = jnp.zeros((ang.shape[0], 64 - ang.shape[1]), F32)
    cos, sin = jnp.cos(ang), jnp.sin(ang)
    return (jnp.concatenate([cos, pad, cos, pad], axis=-1),
            jnp.concatenate([-sin, pad, sin, pad], axis=-1))


def _spread_rope_cols(w):
    half = MLA_ROPE // 2
    z = jnp.zeros(w.shape[:-1] + (64 - half,), w.dtype)
    return jnp.concatenate([w[..., :half], z, w[..., half:], z], axis=-1)


def _mla_mixer(in_proj, b, s, tcx, heads, w_in, q_gain, kv_gain, w_uq, w_ukv):
    rq = q_gain.shape[-1]
    w_in = jnp.concatenate([w_in[:, :2 * rq], _spread_rope_cols(w_in[:, 2 * rq:])], axis=-1).astype(BF16)
    w_uq = w_uq.reshape(rq, heads, HEAD_DIM + MLA_ROPE)
    w_uq = jnp.concatenate([w_uq[..., :HEAD_DIM], _spread_rope_cols(w_uq[..., HEAD_DIM:])],
                           axis=-1).reshape(rq, heads * 2 * HEAD_DIM).astype(BF16)
    w_ukv = w_ukv.astype(BF16)
    tables = _rope_slot_tables(_axial_angles(s, MLA_ROPE))
    scale = (HEAD_DIM + MLA_ROPE) ** -0.5
    zl, zc = in_proj(w_in, F32)
    ropecol = 2 * rq // LANES

    def qkv(z, n_tok, tabs):
        q = _proj(z, q_gain[None], w_uq, kblock=0, out_dtype=BF16).reshape(b, n_tok, -1)
        kv = _proj(z, kv_gain[None], w_ukv, kblock=1, out_dtype=BF16).reshape(b, n_tok, -1)
        q = _prep(q, mode="mla_q", heads=heads, xcol=lambda hh: hh, tables=tabs, scale=scale)
        k = _prep(kv, mode="mla_k", heads=heads, xcol=lambda hh: 2 * hh,
                  x2=z.reshape(b, n_tok, -1), x2col=ropecol, tables=tabs)
        return q, k, kv

    ql, kl, kvl = qkv(zl, s, tables)
    qc, kc, kvc = qkv(zc, tcx, None)
    cols = dict(heads=heads, dq=2 * HEAD_DIM, dv=HEAD_DIM, qcol=lambda hh: hh, kcol=lambda hh: hh,
                vcol=lambda hh: 2 * hh + 1)
    return _flash(ql, kc, kvc, kl, kvl, **cols), _flash(qc, kc, kvc, **cols)


def _gqa_mixer(in_proj, b, s, tcx, heads, w_in, q_gain, k_gain):
    hk = (w_in.shape[-1] // HEAD_DIM - heads) // 2
    grp = heads // hk
    tables = _rope_slot_tables(_axial_angles(s, HEAD_DIM))
    zl, zc = in_proj(w_in.astype(BF16), BF16)
    zl = zl.reshape(b, s, -1)
    zc = zc.reshape(b, tcx, -1)
    scale = HEAD_DIM ** -0.5
    qg, kg = q_gain[None], k_gain[None]
    ql = _prep(zl, mode="head", heads=heads, xcol=lambda hh: hh, gain=qg, tables=tables, scale=scale)
    kl = _prep(zl, mode="head", heads=hk, xcol=lambda hh: heads + hh, gain=kg, tables=tables)
    qc = _prep(zc, mode="head", heads=heads, xcol=lambda hh: hh, gain=qg, scale=scale)
    kc = _prep(zc, mode="head", heads=hk, xcol=lambda hh: heads + hh, gain=kg)
    cols = dict(heads=heads, dq=HEAD_DIM, dv=HEAD_DIM, qcol=lambda hh: hh,
                kcol=lambda hh: hh // grp, vcol=lambda hh: heads + hk + hh // grp)
    return _flash(ql, kc, zc, kl, zl, **cols), _flash(qc, kc, zc, **cols)


def _na_mixer(in_proj, b, s, tcx, heads, w_in, rpb):
    zl, zc = in_proj(w_in.astype(BF16), BF16)
    zl = zl.reshape(b, s, -1)
    zc = zc.reshape(b, tcx, -1)
    y = _na_attention(zl, zc, rpb, heads=heads)
    yc = _flash(zc, zc, zc, heads=heads, dq=HEAD_DIM, dv=HEAD_DIM, qcol=lambda hh: hh,
                kcol=lambda hh: heads + hh, vcol=lambda hh: 2 * heads + hh, scale=HEAD_DIM ** -0.5)
    return y, yc


def _ret_mixer(in_proj, b, s, tcx, w_in, decay_log2):
    rh = decay_log2.shape[-1]
    pos = jnp.arange(s, dtype=F32)
    freqs = ROPE_BASE ** (-jnp.arange(RET_QK // 2, dtype=F32) / (RET_QK // 2))
    ang = pos[:, None] * freqs
    zl, zc = in_proj(w_in.astype(BF16), BF16)
    zl = zl.reshape(b, s, -1)
    zc = zc.reshape(b, tcx, -1)
    o2 = _retention(zl, zc, decay_log2.astype(F32), jnp.cos(ang), jnp.sin(ang), heads=rh)
    return _ret_gate(o2, zl, heads=rh)


def kernel(x, c, ctx, c_ctx, mod_w, mod_b, norm_g, ffn_w_gate, ffn_w_up, ffn_w_down,
           mla_w_in, mla_q_gain, mla_kv_gain, mla_w_uq, mla_w_ukv, mla_w_o,
           gqa_w_in, gqa_q_gain, gqa_k_gain, gqa_w_o,
           na_w_in, na_rpb, na_w_o,
           ret_w_in, ret_decay_log2, ret_w_o):
    b, s, d = x.shape
    tcx = ctx.shape[1]
    depth = mod_w.shape[0]
    assert depth == 4 and s % (NA_QROWS * GRID_W) == 0 and s // GRID_W >= NA_QROWS + NA_WIN_R
    heads = d // HEAD_DIM

    cc = jnp.concatenate([c, c_ctx[None], jnp.zeros((8 - b - 1, d), F32)], axis=0)
    mods = _mod_vectors(cc, mod_w, mod_b).reshape(depth, 8, N_MOD, d)

    h = x.reshape(b * s, d)
    hc = ctx.reshape(b * tcx, d)

    for i in range(depth):
        kind = i % 4
        last = i == depth - 1
        ml = [mods[i, :b, j][:, None, :] for j in range(N_MOD)]
        mc = [mods[i, b:b + 1, j][:, None, :] for j in range(N_MOD)]
        gn = [norm_g[i, j][None, :] for j in range(6)]

        def ffn(hh, m, f):
            return _ffn(hh, m[3 * f], m[3 * f + 1], m[3 * f + 2], gn[2 * f], gn[2 * f + 1],
                        ffn_w_gate[i, f // 2].astype(BF16), ffn_w_up[i, f // 2].astype(BF16),
                        ffn_w_down[i, f // 2].astype(BF16))

        h = ffn(h, ml, 0)
        hc = ffn(hc, mc, 0)

        def in_proj(w, out_dtype):
            zl = _proj(h, gn[2], w, shift=ml[3], scale=ml[4], out_dtype=out_dtype)
            zc = _proj(hc, gn[2], w, shift=mc[3], scale=mc[4], out_dtype=out_dtype)
            return zl, zc

        if kind == 0:
            y, yc = _mla_mixer(in_proj, b, s, tcx, heads, mla_w_in[0], mla_q_gain[0], mla_kv_gain[0],
                               mla_w_uq[0], mla_w_ukv[0])
            w_o = mla_w_o[0]
        elif kind == 1:
            y, yc = _gqa_mixer(in_proj, b, s, tcx, heads, gqa_w_in[0], gqa_q_gain[0], gqa_k_gain[0])
            w_o = gqa_w_o[0]
        elif kind == 2:
            y, yc = _na_mixer(in_proj, b, s, tcx, heads, na_w_in[0], na_rpb[0])
            w_o = na_w_o[0]
        else:
            y, yc = _ret_mixer(in_proj, b, s, tcx, ret_w_in[0], ret_decay_log2[0]), None
            w_o = ret_w_o[0]

        w_o = w_o.astype(BF16)
        h = _out_proj(y.reshape(b * s, -1), w_o, gn[3], ml[5], h)
        h = ffn(h, ml, 2)
        if not last:
            hc = _out_proj(yc.reshape(b * tcx, -1), w_o, gn[3], mc[5], hc)
            hc = ffn(hc, mc, 2)
    return h.reshape(b, s, d)
```

```python
import functools

import jax
import jax.numpy as jnp
from jax import lax
from jax.experimental import pallas as pl
from jax.experimental.pallas import tpu as pltpu

F32 = jnp.float32
BF16 = jnp.bfloat16

RMS_EPS = 1e-6
ROPE_BASE = 10000.0
FFN_RES = 0.5
GRID_W = 64
N_MOD = 9
LANES = 128
HEAD_DIM = 128
MLA_ROPE = 64
RET_QK = 256
RET_V = 512
RET_CHUNK = 512
NA_WIN_R = 8
NA_WIN_C = 16
NA_QROWS = 8
NEG_INF = -1e30
LOG2_E = 1.4426950408889634
VMEM_LIMIT_BYTES = 56 * 1024 * 1024


def _cparams(semantics):
    return pltpu.CompilerParams(dimension_semantics=semantics, vmem_limit_bytes=VMEM_LIMIT_BYTES)


def _tile(n, pref):
    if n <= pref:
        return n
    t = pref
    while n % t:
        t //= 2
    return t


def _rms(x):
    return x * lax.rsqrt(jnp.mean(x * x, axis=-1, keepdims=True) + RMS_EPS)


def _silu(x):
    return x * jax.nn.sigmoid(x)


def _mod_kernel(cc_ref, w_ref, b_ref, o_ref):
    s = _silu(cc_ref[...]).astype(BF16)
    o_ref[0] = jnp.dot(s, w_ref[0].astype(BF16), preferred_element_type=F32) + b_ref[0]


def _mod_vectors(cc, mod_w, mod_b):
    depth, d, n = mod_w.shape
    rows = cc.shape[0]
    tn = _tile(n, 1024)
    return pl.pallas_call(
        _mod_kernel,
        grid=(depth, n // tn),
        in_specs=[
            pl.BlockSpec((rows, d), lambda l, j: (0, 0)),
            pl.BlockSpec((1, d, tn), lambda l, j: (l, 0, j)),
            pl.BlockSpec((1, 1, tn), lambda l, j: (l, 0, j)),
        ],
        out_specs=pl.BlockSpec((1, rows, tn), lambda l, j: (l, 0, j)),
        out_shape=jax.ShapeDtypeStruct((depth, rows, n), F32),
        compiler_params=_cparams(("arbitrary", "arbitrary")),
        name="mod_vectors",
    )(cc, mod_w, mod_b.reshape(depth, 1, n))


def _ffn_kernel(h_ref, sh_ref, sc_ref, gt_ref, gpre_ref, gpost_ref, wg_ref, wu_ref, wd_ref,
                o_ref, u_ref, acc_ref, *, nk):
    k = pl.program_id(1)

    @pl.when(k == 0)
    def _():
        u = _rms(h_ref[...]) * gpre_ref[...]
        u_ref[...] = (u * (1.0 + sc_ref[0]) + sh_ref[0]).astype(BF16)
        acc_ref[...] = jnp.zeros_like(acc_ref)

    u = u_ref[...]
    g = jnp.dot(u, wg_ref[...], preferred_element_type=F32)
    up = jnp.dot(u, wu_ref[...], preferred_element_type=F32)
    a = (_silu(g) * up).astype(BF16)
    acc_ref[...] += jnp.dot(a, wd_ref[...], preferred_element_type=F32)

    @pl.when(k == nk - 1)
    def _():
        y = _rms(acc_ref[...]) * gpost_ref[...]
        o_ref[...] = h_ref[...] + FFN_RES * gt_ref[0] * y


def _ffn(h, shift, scale, gate, g_pre, g_post, wg, wu, wd, *, tm=512, tf=512):
    m, d = h.shape
    f = wg.shape[1]
    bm = shift.shape[0]
    rows_per_b = m // bm
    tm = _tile(rows_per_b, tm)
    tf = _tile(f, tf)
    nk = f // tf
    per_b = rows_per_b // tm
    mod_spec = pl.BlockSpec((1, 1, d), lambda i, k: (i // per_b, 0, 0))
    vec_spec = pl.BlockSpec((1, d), lambda i, k: (0, 0))
    return pl.pallas_call(
        functools.partial(_ffn_kernel, nk=nk),
        grid=(m // tm, nk),
        in_specs=[
            pl.BlockSpec((tm, d), lambda i, k: (i, 0)),
            mod_spec, mod_spec, mod_spec, vec_spec, vec_spec,
            pl.BlockSpec((d, tf), lambda i, k: (0, k)),
            pl.BlockSpec((d, tf), lambda i, k: (0, k)),
            pl.BlockSpec((tf, d), lambda i, k: (k, 0)),
        ],
        out_specs=pl.BlockSpec((tm, d), lambda i, k: (i, 0)),
        out_shape=jax.ShapeDtypeStruct((m, d), F32),
        scratch_shapes=[pltpu.VMEM((tm, d), BF16), pltpu.VMEM((tm, d), F32)],
        compiler_params=_cparams(("parallel", "arbitrary")),
        name="ffn_half_step",
    )(h, shift, scale, gate, g_pre, g_post, wg, wu, wd)


def _proj_kernel(*refs, modulated):
    if modulated:
        x_ref, g_ref, sh_ref, sc_ref, w_ref, o_ref, u_ref = refs
    else:
        x_ref, g_ref, w_ref, o_ref, u_ref = refs

    @pl.when(pl.program_id(1) == 0)
    def _():
        u = _rms(x_ref[...].astype(F32)) * g_ref[...]
        if modulated:
            u = u * (1.0 + sc_ref[0]) + sh_ref[0]
        u_ref[...] = u.astype(BF16)

    o_ref[...] = jnp.dot(u_ref[...], w_ref[...], preferred_element_type=F32).astype(o_ref.dtype)


def _proj(x, gain, w, *, out_dtype, shift=None, scale=None, kblock=0, tm=512):
    m = x.shape[0]
    k, n = w.shape
    modulated = shift is not None
    bm = shift.shape[0] if modulated else 1
    rows_per_b = m // bm
    tm = _tile(rows_per_b, tm)
    per_b = rows_per_b // tm
    tn = n if k * n * 2 <= 6 * 1024 * 1024 else _tile(n, 1024)
    in_specs = [pl.BlockSpec((tm, k), lambda i, j: (i, kblock)),
                pl.BlockSpec((1, k), lambda i, j: (0, 0))]
    args = [x, gain]
    if modulated:
        mod_spec = pl.BlockSpec((1, 1, k), lambda i, j: (i // per_b, 0, 0))
        in_specs += [mod_spec, mod_spec]
        args += [shift, scale]
    in_specs.append(pl.BlockSpec((k, tn), lambda i, j: (0, j)))
    args.append(w)
    return pl.pallas_call(
        functools.partial(_proj_kernel, modulated=modulated),
        grid=(m // tm, n // tn),
        in_specs=in_specs,
        out_specs=pl.BlockSpec((tm, tn), lambda i, j: (i, j)),
        out_shape=jax.ShapeDtypeStruct((m, n), out_dtype),
        scratch_shapes=[pltpu.VMEM((tm, k), BF16)],
        compiler_params=_cparams(("parallel", "arbitrary")),
        name="norm_proj",
    )(*args)


def _out_proj_kernel(a_ref, w_ref, gpost_ref, gt_ref, h_ref, o_ref, acc_ref, *, nk):
    k = pl.program_id(1)

    @pl.when(k == 0)
    def _():
        acc_ref[...] = jnp.zeros_like(acc_ref)

    acc_ref[...] += jnp.dot(a_ref[...], w_ref[...], preferred_element_type=F32)

    @pl.when(k == nk - 1)
    def _():
        o_ref[...] = h_ref[...] + gt_ref[0] * (_rms(acc_ref[...]) * gpost_ref[...])


def _out_proj(a, w, g_post, gate, h, *, tm=512, tk=1024):
    m, kdim = a.shape
    d = w.shape[1]
    bm = gate.shape[0]
    rows_per_b = m // bm
    tm = _tile(rows_per_b, tm)
    per_b = rows_per_b // tm
    tk = _tile(kdim, tk)
    nk = kdim // tk
    return pl.pallas_call(
        functools.partial(_out_proj_kernel, nk=nk),
        grid=(m // tm, nk),
        in_specs=[
            pl.BlockSpec((tm, tk), lambda i, k: (i, k)),
            pl.BlockSpec((tk, d), lambda i, k: (k, 0)),
            pl.BlockSpec((1, d), lambda i, k: (0, 0)),
            pl.BlockSpec((1, 1, d), lambda i, k: (i // per_b, 0, 0)),
            pl.BlockSpec((tm, d), lambda i, k: (i, 0)),
        ],
        out_specs=pl.BlockSpec((tm, d), lambda i, k: (i, 0)),
        out_shape=jax.ShapeDtypeStruct((m, d), F32),
        scratch_shapes=[pltpu.VMEM((tm, d), F32)],
        compiler_params=_cparams(("parallel", "arbitrary")),
        name="out_proj",
    )(a, w, g_post, gate, h)


def _rope128(x, c_ref, s_ref):
    return x * c_ref[...] + pltpu.roll(x, 64, axis=1) * s_ref[...]


def _prep_kernel(*refs, mode, rope, norm, scale, heads, xcol, x2col):
    refs = list(refs)
    o_ref = refs.pop()
    x_ref = refs.pop(0)
    x2_ref = refs.pop(0) if mode == "mla_k" else None
    g_ref = refs.pop(0) if norm else None
    c_ref, s_ref = (refs.pop(0), refs.pop(0)) if rope else (None, None)
    wx = HEAD_DIM if mode in ("head", "mla_k") else 2 * HEAD_DIM
    wo = HEAD_DIM if mode == "head" else 2 * HEAD_DIM

    if mode == "mla_k":
        shared = x2_ref[0, :, pl.ds(x2col * LANES, LANES)].astype(F32)
        if rope:
            shared = _rope128(shared, c_ref, s_ref)
        shared = shared.astype(o_ref.dtype)
    for h in range(heads):
        x = x_ref[0, :, pl.ds(xcol(h) * wx, wx)].astype(F32)
        if mode == "head":
            if norm:
                x = _rms(x) * g_ref[...]
            if rope:
                x = _rope128(x, c_ref, s_ref)
            o_ref[0, :, pl.ds(h * wo, wo)] = (x * scale).astype(o_ref.dtype)
        elif mode == "mla_q":
            r = x[:, HEAD_DIM:]
            if rope:
                r = _rope128(r, c_ref, s_ref)
            o_ref[0, :, pl.ds(h * wo, HEAD_DIM)] = (x[:, :HEAD_DIM] * scale).astype(o_ref.dtype)
            o_ref[0, :, pl.ds(h * wo + HEAD_DIM, HEAD_DIM)] = (r * scale).astype(o_ref.dtype)
        else:
            o_ref[0, :, pl.ds(h * wo, HEAD_DIM)] = x.astype(o_ref.dtype)
            o_ref[0, :, pl.ds(h * wo + HEAD_DIM, HEAD_DIM)] = shared


def _prep(x, *, mode, heads, xcol, x2=None, x2col=None, gain=None, tables=None, scale=1.0, ts=256):
    b, t, _ = x.shape
    ts = _tile(t, ts)
    wo = HEAD_DIM if mode == "head" else 2 * HEAD_DIM
    in_specs = [pl.BlockSpec((1, ts, x.shape[2]), lambda bb, i: (bb, i, 0))]
    args = [x]
    if mode == "mla_k":
        in_specs.append(pl.BlockSpec((1, ts, x2.shape[2]), lambda bb, i: (bb, i, 0)))
        args.append(x2)
    if gain is not None:
        in_specs.append(pl.BlockSpec((1, HEAD_DIM), lambda bb, i: (0, 0)))
        args.append(gain)
    if tables is not None:
        tspec = pl.BlockSpec((ts, LANES), lambda bb, i: (i, 0))
        in_specs += [tspec, tspec]
        args += list(tables)
    return pl.pallas_call(
        functools.partial(_prep_kernel, mode=mode, rope=tables is not None, norm=gain is not None,
                          scale=scale, heads=heads, xcol=xcol, x2col=x2col),
        grid=(b, t // ts),
        in_specs=in_specs,
        out_specs=pl.BlockSpec((1, ts, heads * wo), lambda bb, i: (bb, i, 0)),
        out_shape=jax.ShapeDtypeStruct((b, t, heads * wo), BF16),
        compiler_params=_cparams(("parallel", "parallel")),
        name="head_prep_" + mode,
    )(*args)


def _flash_kernel(*refs, nk, scale, has_lat, n_sub):
    if has_lat:
        q_ref, kc_ref, vc_ref, kl_ref, vl_ref, o_ref, m_ref, acc_ref, va_ref = refs
    else:
        q_ref, kc_ref, vc_ref, o_ref, m_ref, acc_ref, va_ref = refs
    kk = pl.program_id(3)
    tq = q_ref.shape[1]
    dv = o_ref.shape[2]
    tr = tq // n_sub

    def step(k, v):
        tk = k.shape[0]
        va_ref[pl.ds(0, tk), pl.ds(0, dv)] = v
        va = va_ref[pl.ds(0, tk), :]

        def scores(r):
            q = q_ref[0, pl.ds(r * tr, tr), :]
            if scale != 1.0:
                q = (q.astype(F32) * scale).astype(BF16)
            return lax.dot_general(q, k, (((1,), (1,)), ((), ())), preferred_element_type=F32)

        s_next = scores(0)
        for r in range(n_sub):
            rows = pl.ds(r * tr, tr)
            s = s_next
            if r + 1 < n_sub:
                s_next = scores(r + 1)
            m_prev = m_ref[rows, :]
            m_new = jnp.maximum(m_prev, jnp.max(s, axis=-1, keepdims=True))
            alpha = jnp.exp2(m_prev - m_new)
            p = jnp.exp2((s - pltpu.repeat(m_new, tk // LANES, axis=1)).astype(BF16))
            acc_ref[rows, :] = (pltpu.repeat(alpha, 2 * dv // LANES, axis=1) * acc_ref[rows, :]
                                + jnp.dot(p, va, preferred_element_type=F32))
            m_ref[rows, :] = m_new

    @pl.when(kk == 0)
    def _():
        m_ref[...] = jnp.full_like(m_ref, NEG_INF)
        acc_ref[...] = jnp.zeros_like(acc_ref)
        va_ref[:, pl.ds(dv, dv)] = jnp.ones((va_ref.shape[0], dv), va_ref.dtype)
        step(kc_ref[0], vc_ref[0])

    if has_lat:
        @pl.when(kk > 0)
        def _():
            step(kl_ref[0], vl_ref[0])

    @pl.when(kk == nk - 1)
    def _():
        o_ref[0] = (acc_ref[:, pl.ds(0, dv)] / acc_ref[:, pl.ds(dv, dv)]).astype(o_ref.dtype)


def _flash(q, kc, vc, kl=None, vl=None, *, heads, dq, dv, qcol, kcol, vcol, scale=1.0,
           tq=2048, tk=1024, tr=512):
    b, s, _ = q.shape
    tc = kc.shape[1]
    tq = _tile(s, tq)
    has_lat = kl is not None
    if has_lat:
        tk = _tile(kl.shape[1], tk)
        nk = 1 + kl.shape[1] // tk
    else:
        nk = 1
    in_specs = [
        pl.BlockSpec((1, tq, dq), lambda bb, h, i, kk: (bb, i, qcol(h))),
        pl.BlockSpec((1, tc, dq), lambda bb, h, i, kk: (bb, 0, kcol(h))),
        pl.BlockSpec((1, tc, dv), lambda bb, h, i, kk: (bb, 0, vcol(h))),
    ]
    args = [q, kc, vc]
    if has_lat:
        in_specs += [
            pl.BlockSpec((1, tk, dq), lambda bb, h, i, kk: (bb, jnp.maximum(kk - 1, 0), kcol(h))),
            pl.BlockSpec((1, tk, dv), lambda bb, h, i, kk: (bb, jnp.maximum(kk - 1, 0), vcol(h))),
        ]
        args += [kl, vl]
    return pl.pallas_call(
        functools.partial(_flash_kernel, nk=nk, scale=scale, has_lat=has_lat, n_sub=tq // _tile(tq, tr)),
        grid=(b, heads, s // tq, nk),
        in_specs=in_specs,
        out_specs=pl.BlockSpec((1, tq, dv), lambda bb, h, i, kk: (bb, i, h)),
        out_shape=jax.ShapeDtypeStruct((b, s, heads * dv), BF16),
        scratch_shapes=[pltpu.VMEM((tq, LANES), F32), pltpu.VMEM((tq, 2 * dv), F32),
                        pltpu.VMEM((max(tc, tk), 2 * dv), BF16)],
        compiler_params=_cparams(("parallel", "parallel", "parallel", "arbitrary")),
        name="flash_attention",
    )(*args)


def _na_kernel(q_ref, k_ref, v_ref, kc_ref, vc_ref, bias_ref, o_ref, *, band, seq, scale):
    i = pl.program_id(2)
    tq = q_ref.shape[1]
    start = jnp.clip(i * tq - (band - tq) // 2, 0, seq - band)
    start = pl.multiple_of(start, 256)
    q = q_ref[0]
    kb = k_ref[0, pl.ds(start, band), :]
    vb = v_ref[0, pl.ds(start, band), :]
    dn = (((1,), (1,)), ((), ()))
    s_nb = lax.dot_general(q, kb, dn, preferred_element_type=F32) * scale + bias_ref[0, 0]
    s_cx = lax.dot_general(q, kc_ref[0], dn, preferred_element_type=F32) * scale
    m = jnp.maximum(jnp.max(s_nb, axis=-1, keepdims=True), jnp.max(s_cx, axis=-1, keepdims=True))
    p_nb = jnp.exp(s_nb - m)
    p_cx = jnp.exp(s_cx - m)
    l = jnp.sum(p_nb, axis=-1, keepdims=True) + jnp.sum(p_cx, axis=-1, keepdims=True)
    o = (jnp.dot(p_nb.astype(BF16), vb, preferred_element_type=F32)
         + jnp.dot(p_cx.astype(BF16), vc_ref[0], preferred_element_type=F32))
    o_ref[0] = (o / l).astype(o_ref.dtype)


def _na_bias_tiles(rpb, rows):
    nr, wr, wc, w = NA_QROWS, NA_WIN_R, NA_WIN_C, GRID_W
    brows = nr + wr
    heads = rpb.shape[0]
    lpad = w - wc
    vp = jnp.pad(rpb.astype(F32), ((0, 0), (0, 0), (lpad, lpad)), constant_values=NEG_INF)
    toep = jnp.stack([vp[:, :, w - 1 - c: 2 * w - 1 - c] for c in range(w)], axis=2)
    col = jnp.arange(w, dtype=jnp.int32)
    c0 = jnp.clip(col - wc // 2, 0, w - wc)
    in_col = (col[None, :] >= c0[:, None]) & (col[None, :] < c0[:, None] + wc)
    toep = jnp.where(in_col[None, None], toep, NEG_INF)
    masked = jnp.full((heads, 1, w, w), NEG_INF, F32)
    tiles = []
    for r_first, b_first in ((0, 0), (nr, nr - wr // 2), (rows - nr, rows - brows)):
        per_row = []
        for a in range(nr):
            r = r_first + a
            r0 = min(max(r - wr // 2, 0), rows - wr)
            lo = r0 - b_first
            dr_lo = r0 - r + (wr - 1)
            blk = toep[:, dr_lo: dr_lo + wr]
            blk = jnp.concatenate([jnp.tile(masked, (1, lo, 1, 1)), blk,
                                   jnp.tile(masked, (1, brows - wr - lo, 1, 1))], axis=1)
            per_row.append(jnp.swapaxes(blk, 1, 2))
        tiles.append(jnp.stack(per_row, axis=1).reshape(heads, nr * w, brows * w))
    return jnp.stack(tiles, axis=1)


def _na_attention(z, zc, rpb, *, heads):
    b, s, _ = z.shape
    tcx = zc.shape[1]
    rows = s // GRID_W
    tq = NA_QROWS * GRID_W
    band = (NA_QROWS + NA_WIN_R) * GRID_W
    nblk = s // tq
    bias = _na_bias_tiles(rpb, rows)
    scale = HEAD_DIM ** -0.5

    def variant(i):
        return jnp.where(i == 0, 0, jnp.where(i == nblk - 1, 2, 1))

    return pl.pallas_call(
        functools.partial(_na_kernel, band=band, seq=s, scale=scale),
        grid=(b, heads, nblk),
        in_specs=[
            pl.BlockSpec((1, tq, HEAD_DIM), lambda bb, h, i: (bb, i, h)),
            pl.BlockSpec((1, s, HEAD_DIM), lambda bb, h, i: (bb, 0, heads + h)),
            pl.BlockSpec((1, s, HEAD_DIM), lambda bb, h, i: (bb, 0, 2 * heads + h)),
            pl.BlockSpec((1, tcx, HEAD_DIM), lambda bb, h, i: (bb, 0, heads + h)),
            pl.BlockSpec((1, tcx, HEAD_DIM), lambda bb, h, i: (bb, 0, 2 * heads + h)),
            pl.BlockSpec((1, 1, tq, band), lambda bb, h, i: (h, variant(i), 0, 0)),
        ],
        out_specs=pl.BlockSpec((1, tq, HEAD_DIM), lambda bb, h, i: (bb, i, h)),
        out_shape=jax.ShapeDtypeStruct((b, s, heads * HEAD_DIM), BF16),
        compiler_params=_cparams(("parallel", "parallel", "arbitrary")),
        name="neighbourhood_attention",
    )(z, z, z, zc, zc, bias)


def _ret_kernel(dec_ref, ql_ref, kl_ref, vl_ref, kc_ref, vc_ref, cos_ref, sin_ref, o_ref, st_ref, din_ref,
                *, nc):
    h = pl.program_id(1)
    d = pl.program_id(2)
    t = pl.program_id(3)
    half = RET_QK // 2
    lg = jnp.log1p(-jnp.exp2(jnp.full((1, 1), dec_ref[d, h], F32)))
    fwd = d == 0

    def rope(x):
        c, s = cos_ref[...], sin_ref[...]
        x1, x2 = x[:, :half], x[:, half:]
        return jnp.concatenate([x1 * c - x2 * s, x2 * c + x1 * s], axis=-1)

    def update(k, v):
        n = k.shape[0]
        pos = lax.broadcasted_iota(jnp.int32, (n, 1), 0).astype(F32)
        zeta = jnp.exp(jnp.where(fwd, n - 1.0 - pos, pos) * lg)
        kz = (k * zeta).astype(BF16)
        kv = lax.dot_general(kz, v, (((0,), (0,)), ((), ())), preferred_element_type=F32)
        st_ref[...] = st_ref[...] * jnp.exp(n * lg) + kv

    @pl.when(t == 0)
    def _():
        st_ref[...] = jnp.zeros_like(st_ref)
        n = din_ref.shape[0]
        ri = lax.broadcasted_iota(jnp.int32, (n, n), 0)
        ci = lax.broadcasted_iota(jnp.int32, (n, n), 1)
        dist = jnp.where(fwd, ri - ci, ci - ri)
        din_ref[...] = jnp.where(dist >= 0, jnp.exp(jnp.maximum(dist, 0).astype(F32) * lg), 0.0)

    @pl.when(t < nc)
    def _():
        update(kc_ref[0].astype(F32) * RET_QK ** -0.5, vc_ref[0])

    @pl.when(t >= nc)
    def _():
        q = rope(ql_ref[0].astype(F32))
        k = rope(kl_ref[0].astype(F32) * RET_QK ** -0.5)
        v = vl_ref[0]
        n = q.shape[0]
        pos = lax.broadcasted_iota(jnp.int32, (n, 1), 0).astype(F32)
        qb = q.astype(BF16)
        s = lax.dot_general(qb, k.astype(BF16), (((1,), (1,)), ((), ())),
                            preferred_element_type=F32) * din_ref[...]
        xi = jnp.exp(jnp.where(fwd, pos + 1.0, n - pos) * lg)
        o = (jnp.dot(s.astype(BF16), v, preferred_element_type=F32)
             + jnp.dot(qb, st_ref[...].astype(BF16), preferred_element_type=F32) * xi)
        o_ref[0, 0] = o
        update(k, v)


def _retention(z, zc, decay_log2, cos, sin, *, heads):
    b, s, _ = z.shape
    tcx = zc.shape[1]
    chunk = _tile(s, RET_CHUNK)
    cchunk = _tile(tcx, RET_CHUNK)
    nc, nl = tcx // cchunk, s // chunk
    kcol0 = heads
    vcol0 = (2 * heads * RET_QK) // RET_V

    def cidx(d, t):
        c = jnp.minimum(t, nc - 1)
        return jnp.where(d == 0, c, nc - 1 - c)

    def lidx(d, t):
        c = jnp.maximum(t - nc, 0)
        return jnp.where(d == 0, c, nl - 1 - c)

    return pl.pallas_call(
        functools.partial(_ret_kernel, nc=nc),
        grid=(b, heads, 2, nc + nl),
        in_specs=[
            pl.BlockSpec(memory_space=pltpu.SMEM),
            pl.BlockSpec((1, chunk, RET_QK), lambda bb, h, d, t: (bb, lidx(d, t), h)),
            pl.BlockSpec((1, chunk, RET_QK), lambda bb, h, d, t: (bb, lidx(d, t), kcol0 + h)),
            pl.BlockSpec((1, chunk, RET_V), lambda bb, h, d, t: (bb, lidx(d, t), vcol0 + h)),
            pl.BlockSpec((1, cchunk, RET_QK), lambda bb, h, d, t: (bb, cidx(d, t), kcol0 + h)),
            pl.BlockSpec((1, cchunk, RET_V), lambda bb, h, d, t: (bb, cidx(d, t), vcol0 + h)),
            pl.BlockSpec((chunk, RET_QK // 2), lambda bb, h, d, t: (lidx(d, t), 0)),
            pl.BlockSpec((chunk, RET_QK // 2), lambda bb, h, d, t: (lidx(d, t), 0)),
        ],
        out_specs=pl.BlockSpec((1, 1, chunk, RET_V), lambda bb, h, d, t: (d, bb, lidx(d, t), h)),
        out_shape=jax.ShapeDtypeStruct((2, b, s, heads * RET_V), F32),
        scratch_shapes=[pltpu.VMEM((RET_QK, RET_V), F32), pltpu.VMEM((chunk, chunk), F32)],
        compiler_params=_cparams(("parallel", "parallel", "arbitrary", "arbitrary")),
        name="retention_scan",
    )(decay_log2, z, z, z, zc, zc, cos, sin)


def _ret_gate_kernel(of_ref, ob_ref, g_ref, y_ref):
    o = of_ref[0, 0] + ob_ref[0, 0]
    y_ref[0] = (_silu(g_ref[0].astype(F32)) * _rms(o)).astype(y_ref.dtype)


def _ret_gate(o2, z, *, heads, ts=512):
    _, b, s, _ = o2.shape
    ts = _tile(s, ts)
    gcol0 = (2 * heads * RET_QK + heads * RET_V) // RET_V
    return pl.pallas_call(
        _ret_gate_kernel,
        grid=(b, s // ts, heads),
        in_specs=[
            pl.BlockSpec((1, 1, ts, RET_V), lambda bb, i, h: (0, bb, i, h)),
            pl.BlockSpec((1, 1, ts, RET_V), lambda bb, i, h: (1, bb, i, h)),
            pl.BlockSpec((1, ts, RET_V), lambda bb, i, h: (bb, i, gcol0 + h)),
        ],
        out_specs=pl.BlockSpec((1, ts, RET_V), lambda bb, i, h: (bb, i, h)),
        out_shape=jax.ShapeDtypeStruct((b, s, heads * RET_V), BF16),
        compiler_params=_cparams(("parallel", "parallel", "arbitrary")),
        name="retention_gate",
    )(o2, o2, z)


def _axial_angles(n_tokens, rot_dim):
    t = jnp.arange(n_tokens, dtype=jnp.int32)
    row = (t // GRID_W).astype(F32)
    col = (t % GRID_W).astype(F32)
    n_f = rot_dim // 4
    freqs = ROPE_BASE ** (-jnp.arange(n_f, dtype=F32) / n_f)
    return jnp.concatenate([row[:, None] * freqs, col[:, None] * freqs], axis=-1)


def _rope_slot_tables(ang):
    pad = jnp.zeros((ang.shape[0], 64 - ang.shape[1]), F32)
    cos, sin = jnp.cos(ang), jnp.sin(ang)
    return (jnp.concatenate([cos, pad, cos, pad], axis=-1),
            jnp.concatenate([-sin, pad, sin, pad], axis=-1))


def _spread_rope_cols(w):
    half = MLA_ROPE // 2
    z = jnp.zeros(w.shape[:-1] + (64 - half,), w.dtype)
    return jnp.concatenate([w[..., :half], z, w[..., half:], z], axis=-1)


def _mla_mixer(in_proj, b, s, tcx, heads, w_in, q_gain, kv_gain, w_uq, w_ukv):
    rq = q_gain.shape[-1]
    w_in = jnp.concatenate([w_in[:, :2 * rq], _spread_rope_cols(w_in[:, 2 * rq:])], axis=-1).astype(BF16)
    w_uq = w_uq.reshape(rq, heads, HEAD_DIM + MLA_ROPE)
    w_uq = jnp.concatenate([w_uq[..., :HEAD_DIM], _spread_rope_cols(w_uq[..., HEAD_DIM:])],
                           axis=-1).reshape(rq, heads * 2 * HEAD_DIM).astype(BF16)
    w_ukv = w_ukv.astype(BF16)
    tables = _rope_slot_tables(_axial_angles(s, MLA_ROPE))
    scale = (HEAD_DIM + MLA_ROPE) ** -0.5 * LOG2_E
    zl, zc = in_proj(w_in, F32)
    ropecol = 2 * rq // LANES

    def qkv(z, n_tok, tabs):
        q = _proj(z, q_gain[None], w_uq, kblock=0, out_dtype=BF16).reshape(b, n_tok, -1)
        kv = _proj(z, kv_gain[None], w_ukv, kblock=1, out_dtype=BF16).reshape(b, n_tok, -1)
        q = _prep(q, mode="mla_q", heads=heads, xcol=lambda hh: hh, tables=tabs, scale=scale)
        k = _prep(kv, mode="mla_k", heads=heads, xcol=lambda hh: 2 * hh,
                  x2=z.reshape(b, n_tok, -1), x2col=ropecol, tables=tabs)
        return q, k, kv

    ql, kl, kvl = qkv(zl, s, tables)
    qc, kc, kvc = qkv(zc, tcx, None)
    cols = dict(heads=heads, dq=2 * HEAD_DIM, dv=HEAD_DIM, qcol=lambda hh: hh, kcol=lambda hh: hh,
                vcol=lambda hh: 2 * hh + 1)
    return _flash(ql, kc, kvc, kl, kvl, **cols), _flash(qc, kc, kvc, **cols)


def _gqa_mixer(in_proj, b, s, tcx, heads, w_in, q_gain, k_gain):
    hk = (w_in.shape[-1] // HEAD_DIM - heads) // 2
    grp = heads // hk
    tables = _rope_slot_tables(_axial_angles(s, HEAD_DIM))
    zl, zc = in_proj(w_in.astype(BF16), BF16)
    zl = zl.reshape(b, s, -1)
    zc = zc.reshape(b, tcx, -1)
    scale = HEAD_DIM ** -0.5 * LOG2_E
    qg, kg = q_gain[None], k_gain[None]
    ql = _prep(zl, mode="head", heads=heads, xcol=lambda hh: hh, gain=qg, tables=tables, scale=scale)
    kl = _prep(zl, mode="head", heads=hk, xcol=lambda hh: heads + hh, gain=kg, tables=tables)
    qc = _prep(zc, mode="head", heads=heads, xcol=lambda hh: hh, gain=qg, scale=scale)
    kc = _prep(zc, mode="head", heads=hk, xcol=lambda hh: heads + hh, gain=kg)
    cols = dict(heads=heads, dq=HEAD_DIM, dv=HEAD_DIM, qcol=lambda hh: hh,
                kcol=lambda hh: hh // grp, vcol=lambda hh: heads + hk + hh // grp)
    return _flash(ql, kc, zc, kl, zl, **cols), _flash(qc, kc, zc, **cols)


def _na_mixer(in_proj, b, s, tcx, heads, w_in, rpb):
    zl, zc = in_proj(w_in.astype(BF16), BF16)
    zl = zl.reshape(b, s, -1)
    zc = zc.reshape(b, tcx, -1)
    y = _na_attention(zl, zc, rpb, heads=heads)
    yc = _flash(zc, zc, zc, heads=heads, dq=HEAD_DIM, dv=HEAD_DIM, qcol=lambda hh: hh,
                kcol=lambda hh: heads + hh, vcol=lambda hh: 2 * heads + hh,
                scale=HEAD_DIM ** -0.5 * LOG2_E)
    return y, yc


def _ret_mixer(in_proj, b, s, tcx, w_in, decay_log2):
    rh = decay_log2.shape[-1]
    pos = jnp.arange(s, dtype=F32)
    freqs = ROPE_BASE ** (-jnp.arange(RET_QK // 2, dtype=F32) / (RET_QK // 2))
    ang = pos[:, None] * freqs
    zl, zc = in_proj(w_in.astype(BF16), BF16)
    zl = zl.reshape(b, s, -1)
    zc = zc.reshape(b, tcx, -1)
    o2 = _retention(zl, zc, decay_log2.astype(F32), jnp.cos(ang), jnp.sin(ang), heads=rh)
    return _ret_gate(o2, zl, heads=rh)


def kernel(x, c, ctx, c_ctx, mod_w, mod_b, norm_g, ffn_w_gate, ffn_w_up, ffn_w_down,
           mla_w_in, mla_q_gain, mla_kv_gain, mla_w_uq, mla_w_ukv, mla_w_o,
           gqa_w_in, gqa_q_gain, gqa_k_gain, gqa_w_o,
           na_w_in, na_rpb, na_w_o,
           ret_w_in, ret_decay_log2, ret_w_o):
    b, s, d = x.shape
    tcx = ctx.shape[1]
    depth = mod_w.shape[0]
    assert depth == 4 and s % (NA_QROWS * GRID_W) == 0 and s // GRID_W >= NA_QROWS + NA_WIN_R
    heads = d // HEAD_DIM

    cc = jnp.concatenate([c, c_ctx[None], jnp.zeros((8 - b - 1, d), F32)], axis=0)
    mods = _mod_vectors(cc, mod_w, mod_b).reshape(depth, 8, N_MOD, d)

    h = x.reshape(b * s, d)
    hc = ctx.reshape(b * tcx, d)

    for i in range(depth):
        kind = i % 4
        last = i == depth - 1
        ml = [mods[i, :b, j][:, None, :] for j in range(N_MOD)]
        mc = [mods[i, b:b + 1, j][:, None, :] for j in range(N_MOD)]
        gn = [norm_g[i, j][None, :] for j in range(6)]

        def ffn(hh, m, f):
            return _ffn(hh, m[3 * f], m[3 * f + 1], m[3 * f + 2], gn[2 * f], gn[2 * f + 1],
                        ffn_w_gate[i, f // 2].astype(BF16), ffn_w_up[i, f // 2].astype(BF16),
                        ffn_w_down[i, f // 2].astype(BF16))

        h = ffn(h, ml, 0)
        hc = ffn(hc, mc, 0)

        def in_proj(w, out_dtype):
            zl = _proj(h, gn[2], w, shift=ml[3], scale=ml[4], out_dtype=out_dtype)
            zc = _proj(hc, gn[2], w, shift=mc[3], scale=mc[4], out_dtype=out_dtype)
            return zl, zc

        if kind == 0:
            y, yc = _mla_mixer(in_proj, b, s, tcx, heads, mla_w_in[0], mla_q_gain[0], mla_kv_gain[0],
                               mla_w_uq[0], mla_w_ukv[0])
            w_o = mla_w_o[0]
        elif kind == 1:
            y, yc = _gqa_mixer(in_proj, b, s, tcx, heads, gqa_w_in[0], gqa_q_gain[0], gqa_k_gain[0])
            w_o = gqa_w_o[0]
        elif kind == 2:
            y, yc = _na_mixer(in_proj, b, s, tcx, heads, na_w_in[0], na_rpb[0])
            w_o = na_w_o[0]
        else:
            y, yc = _ret_mixer(in_proj, b, s, tcx, ret_w_in[0], ret_decay_log2[0]), None
            w_o = ret_w_o[0]

        w_o = w_o.astype(BF16)
        h = _out_proj(y.reshape(b * s, -1), w_o, gn[3], ml[5], h)
        h = ffn(h, ml, 2)
        if not last:
            hc = _out_proj(yc.reshape(b * tcx, -1), w_o, gn[3], mc[5], hc)
            hc = ffn(hc, mc, 2)
    return h.reshape(b, s, d)
```

```python
import functools

import jax
import jax.numpy as jnp
from jax import lax
from jax.experimental import pallas as pl
from jax.experimental.pallas import tpu as pltpu

F32 = jnp.float32
BF16 = jnp.bfloat16

RMS_EPS = 1e-6
ROPE_BASE = 10000.0
FFN_RES = 0.5
GRID_W = 64
N_MOD = 9
LANES = 128
HEAD_DIM = 128
MLA_ROPE = 64
RET_QK = 256
RET_V = 512
RET_CHUNK = 512
NA_WIN_R = 8
NA_WIN_C = 16
NA_QROWS = 8
NEG_INF = -1e30
LOG2_E = 1.4426950408889634
VMEM_LIMIT_BYTES = 56 * 1024 * 1024


def _cparams(semantics):
    return pltpu.CompilerParams(dimension_semantics=semantics, vmem_limit_bytes=VMEM_LIMIT_BYTES)


def _tile(n, pref):
    if n <= pref:
        return n
    t = pref
    while n % t:
        t //= 2
    return t


def _rms(x):
    return x * lax.rsqrt(jnp.mean(x * x, axis=-1, keepdims=True) + RMS_EPS)


def _silu(x):
    return x * jax.nn.sigmoid(x)


def _lane_tile(x, n):
    return x if n == 1 else jnp.concatenate([x] * n, axis=1)


def _mod_kernel(cc_ref, w_ref, b_ref, o_ref):
    s = _silu(cc_ref[...]).astype(BF16)
    o_ref[0] = jnp.dot(s, w_ref[0].astype(BF16), preferred_element_type=F32) + b_ref[0]


def _mod_vectors(cc, mod_w, mod_b):
    depth, d, n = mod_w.shape
    rows = cc.shape[0]
    tn = _tile(n, 1024)
    return pl.pallas_call(
        _mod_kernel,
        grid=(depth, n // tn),
        in_specs=[
            pl.BlockSpec((rows, d), lambda l, j: (0, 0)),
            pl.BlockSpec((1, d, tn), lambda l, j: (l, 0, j)),
            pl.BlockSpec((1, 1, tn), lambda l, j: (l, 0, j)),
        ],
        out_specs=pl.BlockSpec((1, rows, tn), lambda l, j: (l, 0, j)),
        out_shape=jax.ShapeDtypeStruct((depth, rows, n), F32),
        compiler_params=_cparams(("arbitrary", "arbitrary")),
        name="mod_vectors",
    )(cc, mod_w, mod_b.reshape(depth, 1, n))


def _ffn_kernel(h_ref, sh_ref, sc_ref, gt_ref, gpre_ref, gpost_ref, wg_ref, wu_ref, wd_ref,
                o_ref, u_ref, acc_ref, *, nk):
    k = pl.program_id(1)

    @pl.when(k == 0)
    def _():
        u = _rms(h_ref[...]) * gpre_ref[...]
        u_ref[...] = (u * (1.0 + sc_ref[0]) + sh_ref[0]).astype(BF16)
        acc_ref[...] = jnp.zeros_like(acc_ref)

    u = u_ref[...]
    g = jnp.dot(u, wg_ref[...], preferred_element_type=F32)
    up = jnp.dot(u, wu_ref[...], preferred_element_type=F32)
    a = (_silu(g) * up).astype(BF16)
    acc_ref[...] += jnp.dot(a, wd_ref[...], preferred_element_type=F32)

    @pl.when(k == nk - 1)
    def _():
        y = _rms(acc_ref[...]) * gpost_ref[...]
        o_ref[...] = h_ref[...] + FFN_RES * gt_ref[0] * y


def _ffn(h, shift, scale, gate, g_pre, g_post, wg, wu, wd, *, tm=512, tf=512):
    m, d = h.shape
    f = wg.shape[1]
    bm = shift.shape[0]
    rows_per_b = m // bm
    tm = _tile(rows_per_b, tm)
    tf = _tile(f, tf)
    nk = f // tf
    per_b = rows_per_b // tm
    mod_spec = pl.BlockSpec((1, 1, d), lambda i, k: (i // per_b, 0, 0))
    vec_spec = pl.BlockSpec((1, d), lambda i, k: (0, 0))
    return pl.pallas_call(
        functools.partial(_ffn_kernel, nk=nk),
        grid=(m // tm, nk),
        in_specs=[
            pl.BlockSpec((tm, d), lambda i, k: (i, 0)),
            mod_spec, mod_spec, mod_spec, vec_spec, vec_spec,
            pl.BlockSpec((d, tf), lambda i, k: (0, k)),
            pl.BlockSpec((d, tf), lambda i, k: (0, k)),
            pl.BlockSpec((tf, d), lambda i, k: (k, 0)),
        ],
        out_specs=pl.BlockSpec((tm, d), lambda i, k: (i, 0)),
        out_shape=jax.ShapeDtypeStruct((m, d), F32),
        scratch_shapes=[pltpu.VMEM((tm, d), BF16), pltpu.VMEM((tm, d), F32)],
        compiler_params=_cparams(("parallel", "arbitrary")),
        name="ffn_half_step",
    )(h, shift, scale, gate, g_pre, g_post, wg, wu, wd)


def _proj_kernel(*refs, modulated):
    if modulated:
        x_ref, g_ref, sh_ref, sc_ref, w_ref, o_ref, u_ref = refs
    else:
        x_ref, g_ref, w_ref, o_ref, u_ref = refs

    @pl.when(pl.program_id(1) == 0)
    def _():
        u = _rms(x_ref[...].astype(F32)) * g_ref[...]
        if modulated:
            u = u * (1.0 + sc_ref[0]) + sh_ref[0]
        u_ref[...] = u.astype(BF16)

    o_ref[...] = jnp.dot(u_ref[...], w_ref[...], preferred_element_type=F32).astype(o_ref.dtype)


def _proj(x, gain, w, *, out_dtype, shift=None, scale=None, kblock=0, tm=512):
    m = x.shape[0]
    k, n = w.shape
    modulated = shift is not None
    bm = shift.shape[0] if modulated else 1
    rows_per_b = m // bm
    tm = _tile(rows_per_b, tm)
    per_b = rows_per_b // tm
    tn = n if k * n * 2 <= 6 * 1024 * 1024 else _tile(n, 1024)
    in_specs = [pl.BlockSpec((tm, k), lambda i, j: (i, kblock)),
                pl.BlockSpec((1, k), lambda i, j: (0, 0))]
    args = [x, gain]
    if modulated:
        mod_spec = pl.BlockSpec((1, 1, k), lambda i, j: (i // per_b, 0, 0))
        in_specs += [mod_spec, mod_spec]
        args += [shift, scale]
    in_specs.append(pl.BlockSpec((k, tn), lambda i, j: (0, j)))
    args.append(w)
    return pl.pallas_call(
        functools.partial(_proj_kernel, modulated=modulated),
        grid=(m // tm, n // tn),
        in_specs=in_specs,
        out_specs=pl.BlockSpec((tm, tn), lambda i, j: (i, j)),
        out_shape=jax.ShapeDtypeStruct((m, n), out_dtype),
        scratch_shapes=[pltpu.VMEM((tm, k), BF16)],
        compiler_params=_cparams(("parallel", "arbitrary")),
        name="norm_proj",
    )(*args)


def _out_proj_kernel(a_ref, w_ref, gpost_ref, gt_ref, h_ref, o_ref, acc_ref, *, nk):
    k = pl.program_id(1)

    @pl.when(k == 0)
    def _():
        acc_ref[...] = jnp.zeros_like(acc_ref)

    acc_ref[...] += jnp.dot(a_ref[...], w_ref[...], preferred_element_type=F32)

    @pl.when(k == nk - 1)
    def _():
        o_ref[...] = h_ref[...] + gt_ref[0] * (_rms(acc_ref[...]) * gpost_ref[...])


def _out_proj(a, w, g_post, gate, h, *, tm=512, tk=2048):
    m, kdim = a.shape
    d = w.shape[1]
    bm = gate.shape[0]
    rows_per_b = m // bm
    tm = _tile(rows_per_b, tm)
    per_b = rows_per_b // tm
    tk = _tile(kdim, tk)
    nk = kdim // tk
    return pl.pallas_call(
        functools.partial(_out_proj_kernel, nk=nk),
        grid=(m // tm, nk),
        in_specs=[
            pl.BlockSpec((tm, tk), lambda i, k: (i, k)),
            pl.BlockSpec((tk, d), lambda i, k: (k, 0)),
            pl.BlockSpec((1, d), lambda i, k: (0, 0)),
            pl.BlockSpec((1, 1, d), lambda i, k: (i // per_b, 0, 0)),
            pl.BlockSpec((tm, d), lambda i, k: (i, 0)),
        ],
        out_specs=pl.BlockSpec((tm, d), lambda i, k: (i, 0)),
        out_shape=jax.ShapeDtypeStruct((m, d), F32),
        scratch_shapes=[pltpu.VMEM((tm, d), F32)],
        compiler_params=_cparams(("parallel", "arbitrary")),
        name="out_proj",
    )(a, w, g_post, gate, h)


def _rope128(x, c_ref, s_ref):
    return x * c_ref[...] + pltpu.roll(x, 64, axis=1) * s_ref[...]


def _prep_kernel(*refs, mode, rope, norm, scale, heads, xcol, x2col):
    refs = list(refs)
    o_ref = refs.pop()
    x_ref = refs.pop(0)
    x2_ref = refs.pop(0) if mode == "mla_k" else None
    g_ref = refs.pop(0) if norm else None
    c_ref, s_ref = (refs.pop(0), refs.pop(0)) if rope else (None, None)
    wx = HEAD_DIM if mode in ("head", "mla_k") else 2 * HEAD_DIM
    wo = HEAD_DIM if mode == "head" else 2 * HEAD_DIM

    if mode == "mla_k":
        shared = x2_ref[0, :, pl.ds(x2col * LANES, LANES)].astype(F32)
        if rope:
            shared = _rope128(shared, c_ref, s_ref)
        shared = shared.astype(o_ref.dtype)
    for h in range(heads):
        x = x_ref[0, :, pl.ds(xcol(h) * wx, wx)].astype(F32)
        if mode == "head":
            if norm:
                x = _rms(x) * g_ref[...]
            if rope:
                x = _rope128(x, c_ref, s_ref)
            o_ref[0, :, pl.ds(h * wo, wo)] = (x * scale).astype(o_ref.dtype)
        elif mode == "mla_q":
            r = x[:, HEAD_DIM:]
            if rope:
                r = _rope128(r, c_ref, s_ref)
            o_ref[0, :, pl.ds(h * wo, HEAD_DIM)] = (x[:, :HEAD_DIM] * scale).astype(o_ref.dtype)
            o_ref[0, :, pl.ds(h * wo + HEAD_DIM, HEAD_DIM)] = (r * scale).astype(o_ref.dtype)
        else:
            o_ref[0, :, pl.ds(h * wo, HEAD_DIM)] = x.astype(o_ref.dtype)
            o_ref[0, :, pl.ds(h * wo + HEAD_DIM, HEAD_DIM)] = shared


def _prep(x, *, mode, heads, xcol, x2=None, x2col=None, gain=None, tables=None, scale=1.0, ts=256):
    b, t, _ = x.shape
    ts = _tile(t, ts)
    wo = HEAD_DIM if mode == "head" else 2 * HEAD_DIM
    in_specs = [pl.BlockSpec((1, ts, x.shape[2]), lambda bb, i: (bb, i, 0))]
    args = [x]
    if mode == "mla_k":
        in_specs.append(pl.BlockSpec((1, ts, x2.shape[2]), lambda bb, i: (bb, i, 0)))
        args.append(x2)
    if gain is not None:
        in_specs.append(pl.BlockSpec((1, HEAD_DIM), lambda bb, i: (0, 0)))
        args.append(gain)
    if tables is not None:
        tspec = pl.BlockSpec((ts, LANES), lambda bb, i: (i, 0))
        in_specs += [tspec, tspec]
        args += list(tables)
    return pl.pallas_call(
        functools.partial(_prep_kernel, mode=mode, rope=tables is not None, norm=gain is not None,
                          scale=scale, heads=heads, xcol=xcol, x2col=x2col),
        grid=(b, t // ts),
        in_specs=in_specs,
        out_specs=pl.BlockSpec((1, ts, heads * wo), lambda bb, i: (bb, i, 0)),
        out_shape=jax.ShapeDtypeStruct((b, t, heads * wo), BF16),
        compiler_params=_cparams(("parallel", "parallel")),
        name="head_prep_" + mode,
    )(*args)


def _flash_kernel(*refs, nk, scale, has_lat, n_sub):
    if has_lat:
        q_ref, kc_ref, vc_ref, kl_ref, vl_ref, o_ref, m_ref, acc_ref, va_ref = refs
    else:
        q_ref, kc_ref, vc_ref, o_ref, m_ref, acc_ref, va_ref = refs
    kk = pl.program_id(3)
    tq = q_ref.shape[1]
    dv = o_ref.shape[2]
    tr = tq // n_sub

    def step(k, v):
        tk = k.shape[0]
        va_ref[pl.ds(0, tk), pl.ds(0, dv)] = v
        va = va_ref[pl.ds(0, tk), :]

        def scores(r):
            q = q_ref[0, pl.ds(r * tr, tr), :]
            if scale != 1.0:
                q = (q.astype(F32) * scale).astype(BF16)
            return lax.dot_general(q, k, (((1,), (1,)), ((), ())), preferred_element_type=F32)

        s_next = scores(0)
        for r in range(n_sub):
            rows = pl.ds(r * tr, tr)
            s = s_next
            if r + 1 < n_sub:
                s_next = scores(r + 1)
            m_prev = m_ref[rows, :]
            m_new = jnp.maximum(m_prev, jnp.max(s, axis=-1, keepdims=True))
            alpha = jnp.exp2(m_prev - m_new)
            p = jnp.exp2((s - _lane_tile(m_new, tk // LANES)).astype(BF16))
            acc_ref[rows, :] = (_lane_tile(alpha, 2 * dv // LANES) * acc_ref[rows, :]
                                + jnp.dot(p, va, preferred_element_type=F32))
            m_ref[rows, :] = m_new

    @pl.when(kk == 0)
    def _():
        m_ref[...] = jnp.full_like(m_ref, NEG_INF)
        acc_ref[...] = jnp.zeros_like(acc_ref)
        va_ref[:, pl.ds(dv, dv)] = jnp.ones((va_ref.shape[0], dv), va_ref.dtype)
        step(kc_ref[0], vc_ref[0])

    if has_lat:
        @pl.when(kk > 0)
        def _():
            step(kl_ref[0], vl_ref[0])

    @pl.when(kk == nk - 1)
    def _():
        o_ref[0] = (acc_ref[:, pl.ds(0, dv)] / acc_ref[:, pl.ds(dv, dv)]).astype(o_ref.dtype)


def _flash(q, kc, vc, kl=None, vl=None, *, heads, dq, dv, qcol, kcol, vcol, scale=1.0,
           tq=4096, tk=2048, tr=512):
    b, s, _ = q.shape
    tc = kc.shape[1]
    tq = _tile(s, tq)
    has_lat = kl is not None
    if has_lat:
        tk = _tile(kl.shape[1], tk)
        nk = 1 + kl.shape[1] // tk
    else:
        nk = 1
    in_specs = [
        pl.BlockSpec((1, tq, dq), lambda bb, h, i, kk: (bb, i, qcol(h))),
        pl.BlockSpec((1, tc, dq), lambda bb, h, i, kk: (bb, 0, kcol(h))),
        pl.BlockSpec((1, tc, dv), lambda bb, h, i, kk: (bb, 0, vcol(h))),
    ]
    args = [q, kc, vc]
    if has_lat:
        in_specs += [
            pl.BlockSpec((1, tk, dq), lambda bb, h, i, kk: (bb, jnp.maximum(kk - 1, 0), kcol(h))),
            pl.BlockSpec((1, tk, dv), lambda bb, h, i, kk: (bb, jnp.maximum(kk - 1, 0), vcol(h))),
        ]
        args += [kl, vl]
    return pl.pallas_call(
        functools.partial(_flash_kernel, nk=nk, scale=scale, has_lat=has_lat, n_sub=tq // _tile(tq, tr)),
        grid=(b, heads, s // tq, nk),
        in_specs=in_specs,
        out_specs=pl.BlockSpec((1, tq, dv), lambda bb, h, i, kk: (bb, i, h)),
        out_shape=jax.ShapeDtypeStruct((b, s, heads * dv), BF16),
        scratch_shapes=[pltpu.VMEM((tq, LANES), F32), pltpu.VMEM((tq, 2 * dv), F32),
                        pltpu.VMEM((max(tc, tk), 2 * dv), BF16)],
        compiler_params=_cparams(("parallel", "parallel", "parallel", "arbitrary")),
        name="flash_attention",
    )(*args)


def _na_kernel(q_ref, k_ref, v_ref, kc_ref, vc_ref, bias_ref, o_ref, *, band, seq, scale, n_sub):
    i = pl.program_id(2)
    tq = q_ref.shape[1]
    start = jnp.clip(i * tq - (band - tq) // 2, 0, seq - band)
    start = pl.multiple_of(start, 256)
    dv = o_ref.shape[2]
    kb = k_ref[0, pl.ds(start, band), :]
    kc = kc_ref[0]
    vb = jnp.concatenate([v_ref[0, pl.ds(start, band), :], jnp.ones((band, dv), BF16)], axis=1)
    vc = jnp.concatenate([vc_ref[0], jnp.ones((kc.shape[0], dv), BF16)], axis=1)
    dn = (((1,), (1,)), ((), ()))
    tr = tq // n_sub
    for r in range(n_sub):
        rows = pl.ds(r * tr, tr)
        q = (q_ref[0, rows, :].astype(F32) * scale).astype(BF16)
        s_nb = lax.dot_general(q, kb, dn, preferred_element_type=F32) + bias_ref[0, 0, rows, :]
        s_cx = lax.dot_general(q, kc, dn, preferred_element_type=F32)
        m = jnp.maximum(jnp.max(s_nb, axis=-1, keepdims=True), jnp.max(s_cx, axis=-1, keepdims=True))
        p_nb = jnp.exp2((s_nb - m).astype(BF16))
        p_cx = jnp.exp2((s_cx - m).astype(BF16))
        o = (jnp.dot(p_nb, vb, preferred_element_type=F32) + jnp.dot(p_cx, vc, preferred_element_type=F32))
        o_ref[0, rows, :] = (o[:, :dv] / o[:, dv:]).astype(o_ref.dtype)


def _na_bias_tiles(rpb, rows):
    nr, wr, wc, w = NA_QROWS, NA_WIN_R, NA_WIN_C, GRID_W
    brows = nr + wr
    heads = rpb.shape[0]
    lpad = w - wc
    vp = jnp.pad(rpb.astype(F32) * LOG2_E, ((0, 0), (0, 0), (lpad, lpad)), constant_values=NEG_INF)
    toep = jnp.stack([vp[:, :, w - 1 - c: 2 * w - 1 - c] for c in range(w)], axis=2)
    col = jnp.arange(w, dtype=jnp.int32)
    c0 = jnp.clip(col - wc // 2, 0, w - wc)
    in_col = (col[None, :] >= c0[:, None]) & (col[None, :] < c0[:, None] + wc)
    toep = jnp.where(in_col[None, None], toep, NEG_INF)
    masked = jnp.full((heads, 1, w, w), NEG_INF, F32)
    tiles = []
    for r_first, b_first in ((0, 0), (nr, nr - wr // 2), (rows - nr, rows - brows)):
        per_row = []
        for a in range(nr):
            r = r_first + a
            r0 = min(max(r - wr // 2, 0), rows - wr)
            lo = r0 - b_first
            dr_lo = r0 - r + (wr - 1)
            blk = toep[:, dr_lo: dr_lo + wr]
            blk = jnp.concatenate([jnp.tile(masked, (1, lo, 1, 1)), blk,
                                   jnp.tile(masked, (1, brows - wr - lo, 1, 1))], axis=1)
            per_row.append(jnp.swapaxes(blk, 1, 2))
        tiles.append(jnp.stack(per_row, axis=1).reshape(heads, nr * w, brows * w))
    return jnp.stack(tiles, axis=1)


def _na_attention(z, zc, rpb, *, heads):
    b, s, _ = z.shape
    tcx = zc.shape[1]
    rows = s // GRID_W
    tq = NA_QROWS * GRID_W
    band = (NA_QROWS + NA_WIN_R) * GRID_W
    nblk = s // tq
    bias = _na_bias_tiles(rpb, rows)
    scale = HEAD_DIM ** -0.5 * LOG2_E

    def variant(i):
        return jnp.where(i == 0, 0, jnp.where(i == nblk - 1, 2, 1))

    return pl.pallas_call(
        functools.partial(_na_kernel, band=band, seq=s, scale=scale, n_sub=2),
        grid=(b, heads, nblk),
        in_specs=[
            pl.BlockSpec((1, tq, HEAD_DIM), lambda bb, h, i: (bb, i, h)),
            pl.BlockSpec((1, s, HEAD_DIM), lambda bb, h, i: (bb, 0, heads + h)),
            pl.BlockSpec((1, s, HEAD_DIM), lambda bb, h, i: (bb, 0, 2 * heads + h)),
            pl.BlockSpec((1, tcx, HEAD_DIM), lambda bb, h, i: (bb, 0, heads + h)),
            pl.BlockSpec((1, tcx, HEAD_DIM), lambda bb, h, i: (bb, 0, 2 * heads + h)),
            pl.BlockSpec((1, 1, tq, band), lambda bb, h, i: (h, variant(i), 0, 0)),
        ],
        out_specs=pl.BlockSpec((1, tq, HEAD_DIM), lambda bb, h, i: (bb, i, h)),
        out_shape=jax.ShapeDtypeStruct((b, s, heads * HEAD_DIM), BF16),
        compiler_params=_cparams(("parallel", "parallel", "arbitrary")),
        name="neighbourhood_attention",
    )(z, z, z, zc, zc, bias)


def _ret_kernel(dec_ref, ql_ref, kl_ref, vl_ref, kc_ref, vc_ref, cos_ref, sin_ref, o_ref, st_ref, din_ref,
                *, nc):
    h = pl.program_id(1)
    d = pl.program_id(2)
    t = pl.program_id(3)
    half = RET_QK // 2
    lg = jnp.log1p(-jnp.exp2(jnp.full((1, 1), dec_ref[d, h], F32)))
    fwd = d == 0

    def rope(x):
        c, s = cos_ref[...], sin_ref[...]
        x1, x2 = x[:, :half], x[:, half:]
        return jnp.concatenate([x1 * c - x2 * s, x2 * c + x1 * s], axis=-1)

    def update(k, v):
        n = k.shape[0]
        pos = lax.broadcasted_iota(jnp.int32, (n, 1), 0).astype(F32)
        zeta = jnp.exp(jnp.where(fwd, n - 1.0 - pos, pos) * lg)
        kz = (k * zeta).astype(BF16)
        kv = lax.dot_general(kz, v, (((0,), (0,)), ((), ())), preferred_element_type=F32)
        st_ref[...] = st_ref[...] * jnp.exp(n * lg) + kv

    @pl.when(t == 0)
    def _():
        st_ref[...] = jnp.zeros_like(st_ref)
        n = din_ref.shape[0]
        ri = lax.broadcasted_iota(jnp.int32, (n, n), 0)
        ci = lax.broadcasted_iota(jnp.int32, (n, n), 1)
        dist = jnp.where(fwd, ri - ci, ci - ri)
        din_ref[...] = jnp.where(dist >= 0, jnp.exp(jnp.maximum(dist, 0).astype(F32) * lg), 0.0)

    @pl.when(t < nc)
    def _():
        update(kc_ref[0].astype(F32) * RET_QK ** -0.5, vc_ref[0])

    @pl.when(t >= nc)
    def _():
        q = rope(ql_ref[0].astype(F32))
        k = rope(kl_ref[0].astype(F32) * RET_QK ** -0.5)
        v = vl_ref[0]
        n = q.shape[0]
        pos = lax.broadcasted_iota(jnp.int32, (n, 1), 0).astype(F32)
        qb = q.astype(BF16)
        s = lax.dot_general(qb, k.astype(BF16), (((1,), (1,)), ((), ())),
                            preferred_element_type=F32) * din_ref[...]
        xi = jnp.exp(jnp.where(fwd, pos + 1.0, n - pos) * lg)
        o = (jnp.dot(s.astype(BF16), v, preferred_element_type=F32)
             + jnp.dot(qb, st_ref[...].astype(BF16), preferred_element_type=F32) * xi)
        o_ref[0, 0] = o
        update(k, v)


def _retention(z, zc, decay_log2, cos, sin, *, heads):
    b, s, _ = z.shape
    tcx = zc.shape[1]
    chunk = _tile(s, RET_CHUNK)
    cchunk = _tile(tcx, RET_CHUNK)
    nc, nl = tcx // cchunk, s // chunk
    kcol0 = heads
    vcol0 = (2 * heads * RET_QK) // RET_V

    def cidx(d, t):
        c = jnp.minimum(t, nc - 1)
        return jnp.where(d == 0, c, nc - 1 - c)

    def lidx(d, t):
        c = jnp.maximum(t - nc, 0)
        return jnp.where(d == 0, c, nl - 1 - c)

    return pl.pallas_call(
        functools.partial(_ret_kernel, nc=nc),
        grid=(b, heads, 2, nc + nl),
        in_specs=[
            pl.BlockSpec(memory_space=pltpu.SMEM),
            pl.BlockSpec((1, chunk, RET_QK), lambda bb, h, d, t: (bb, lidx(d, t), h)),
            pl.BlockSpec((1, chunk, RET_QK), lambda bb, h, d, t: (bb, lidx(d, t), kcol0 + h)),
            pl.BlockSpec((1, chunk, RET_V), lambda bb, h, d, t: (bb, lidx(d, t), vcol0 + h)),
            pl.BlockSpec((1, cchunk, RET_QK), lambda bb, h, d, t: (bb, cidx(d, t), kcol0 + h)),
            pl.BlockSpec((1, cchunk, RET_V), lambda bb, h, d, t: (bb, cidx(d, t), vcol0 + h)),
            pl.BlockSpec((chunk, RET_QK // 2), lambda bb, h, d, t: (lidx(d, t), 0)),
            pl.BlockSpec((chunk, RET_QK // 2), lambda bb, h, d, t: (lidx(d, t), 0)),
        ],
        out_specs=pl.BlockSpec((1, 1, chunk, RET_V), lambda bb, h, d, t: (d, bb, lidx(d, t), h)),
        out_shape=jax.ShapeDtypeStruct((2, b, s, heads * RET_V), F32),
        scratch_shapes=[pltpu.VMEM((RET_QK, RET_V), F32), pltpu.VMEM((chunk, chunk), F32)],
        compiler_params=_cparams(("parallel", "parallel", "arbitrary", "arbitrary")),
        name="retention_scan",
    )(decay_log2, z, z, z, zc, zc, cos, sin)


def _ret_gate_kernel(of_ref, ob_ref, g_ref, y_ref):
    o = of_ref[0, 0] + ob_ref[0, 0]
    y_ref[0] = (_silu(g_ref[0].astype(F32)) * _rms(o)).astype(y_ref.dtype)


def _ret_gate(o2, z, *, heads, ts=512):
    _, b, s, _ = o2.shape
    ts = _tile(s, ts)
    gcol0 = (2 * heads * RET_QK + heads * RET_V) // RET_V
    return pl.pallas_call(
        _ret_gate_kernel,
        grid=(b, s // ts, heads),
        in_specs=[
            pl.BlockSpec((1, 1, ts, RET_V), lambda bb, i, h: (0, bb, i, h)),
            pl.BlockSpec((1, 1, ts, RET_V), lambda bb, i, h: (1, bb, i, h)),
            pl.BlockSpec((1, ts, RET_V), lambda bb, i, h: (bb, i, gcol0 + h)),
        ],
        out_specs=pl.BlockSpec((1, ts, RET_V), lambda bb, i, h: (bb, i, h)),
        out_shape=jax.ShapeDtypeStruct((b, s, heads * RET_V), BF16),
        compiler_params=_cparams(("parallel", "parallel", "arbitrary")),
        name="retention_gate",
    )(o2, o2, z)


def _axial_angles(n_tokens, rot_dim):
    t = jnp.arange(n_tokens, dtype=jnp.int32)
    row = (t // GRID_W).astype(F32)
    col = (t % GRID_W).astype(F32)
    n_f = rot_dim // 4
    freqs = ROPE_BASE ** (-jnp.arange(n_f, dtype=F32) / n_f)
    return jnp.concatenate([row[:, None] * freqs, col[:, None] * freqs], axis=-1)


def _rope_slot_tables(ang):
    pad = jnp.zeros((ang.shape[0], 64 - ang.shape[1]), F32)
    cos, sin = jnp.cos(ang), jnp.sin(ang)
    return (jnp.concatenate([cos, pad, cos, pad], axis=-1),
            jnp.concatenate([-sin, pad, sin, pad], axis=-1))


def _spread_rope_cols(w):
    half = MLA_ROPE // 2
    z = jnp.zeros(w.shape[:-1] + (64 - half,), w.dtype)
    return jnp.concatenate([w[..., :half], z, w[..., half:], z], axis=-1)


def _mla_mixer(in_proj, b, s, tcx, heads, w_in, q_gain, kv_gain, w_uq, w_ukv):
    rq = q_gain.shape[-1]
    w_in = jnp.concatenate([w_in[:, :2 * rq], _spread_rope_cols(w_in[:, 2 * rq:])], axis=-1).astype(BF16)
    w_uq = w_uq.reshape(rq, heads, HEAD_DIM + MLA_ROPE)
    w_uq = jnp.concatenate([w_uq[..., :HEAD_DIM], _spread_rope_cols(w_uq[..., HEAD_DIM:])],
                           axis=-1).reshape(rq, heads * 2 * HEAD_DIM).astype(BF16)
    w_ukv = w_ukv.astype(BF16)
    tables = _rope_slot_tables(_axial_angles(s, MLA_ROPE))
    scale = (HEAD_DIM + MLA_ROPE) ** -0.5 * LOG2_E
    zl, zc = in_proj(w_in, F32)
    ropecol = 2 * rq // LANES

    def qkv(z, n_tok, tabs):
        q = _proj(z, q_gain[None], w_uq, kblock=0, out_dtype=BF16).reshape(b, n_tok, -1)
        kv = _proj(z, kv_gain[None], w_ukv, kblock=1, out_dtype=BF16).reshape(b, n_tok, -1)
        q = _prep(q, mode="mla_q", heads=heads, xcol=lambda hh: hh, tables=tabs, scale=scale)
        k = _prep(kv, mode="mla_k", heads=heads, xcol=lambda hh: 2 * hh,
                  x2=z.reshape(b, n_tok, -1), x2col=ropecol, tables=tabs)
        return q, k, kv

    ql, kl, kvl = qkv(zl, s, tables)
    qc, kc, kvc = qkv(zc, tcx, None)
    cols = dict(heads=heads, dq=2 * HEAD_DIM, dv=HEAD_DIM, qcol=lambda hh: hh, kcol=lambda hh: hh,
                vcol=lambda hh: 2 * hh + 1)
    return _flash(ql, kc, kvc, kl, kvl, **cols), _flash(qc, kc, kvc, **cols)


def _gqa_mixer(in_proj, b, s, tcx, heads, w_in, q_gain, k_gain):
    hk = (w_in.shape[-1] // HEAD_DIM - heads) // 2
    grp = heads // hk
    tables = _rope_slot_tables(_axial_angles(s, HEAD_DIM))
    zl, zc = in_proj(w_in.astype(BF16), BF16)
    zl = zl.reshape(b, s, -1)
    zc = zc.reshape(b, tcx, -1)
    scale = HEAD_DIM ** -0.5 * LOG2_E
    qg, kg = q_gain[None], k_gain[None]
    ql = _prep(zl, mode="head", heads=heads, xcol=lambda hh: hh, gain=qg, tables=tables, scale=scale)
    kl = _prep(zl, mode="head", heads=hk, xcol=lambda hh: heads + hh, gain=kg, tables=tables)
    qc = _prep(zc, mode="head", heads=heads, xcol=lambda hh: hh, gain=qg, scale=scale)
    kc = _prep(zc, mode="head", heads=hk, xcol=lambda hh: heads + hh, gain=kg)
    cols = dict(heads=heads, dq=HEAD_DIM, dv=HEAD_DIM, qcol=lambda hh: hh,
                kcol=lambda hh: hh // grp, vcol=lambda hh: heads + hk + hh // grp)
    return _flash(ql, kc, zc, kl, zl, **cols), _flash(qc, kc, zc, **cols)


def _na_mixer(in_proj, b, s, tcx, heads, w_in, rpb):
    zl, zc = in_proj(w_in.astype(BF16), BF16)
    zl = zl.reshape(b, s, -1)
    zc = zc.reshape(b, tcx, -1)
    y = _na_attention(zl, zc, rpb, heads=heads)
    yc = _flash(zc, zc, zc, heads=heads, dq=HEAD_DIM, dv=HEAD_DIM, qcol=lambda hh: hh,
                kcol=lambda hh: heads + hh, vcol=lambda hh: 2 * heads + hh,
                scale=HEAD_DIM ** -0.5 * LOG2_E)
    return y, yc


def _ret_mixer(in_proj, b, s, tcx, w_in, decay_log2):
    rh = decay_log2.shape[-1]
    pos = jnp.arange(s, dtype=F32)
    freqs = ROPE_BASE ** (-jnp.arange(RET_QK // 2, dtype=F32) / (RET_QK // 2))
    ang = pos[:, None] * freqs
    zl, zc = in_proj(w_in.astype(BF16), BF16)
    zl = zl.reshape(b, s, -1)
    zc = zc.reshape(b, tcx, -1)
    o2 = _retention(zl, zc, decay_log2.astype(F32), jnp.cos(ang), jnp.sin(ang), heads=rh)
    return _ret_gate(o2, zl, heads=rh)


def kernel(x, c, ctx, c_ctx, mod_w, mod_b, norm_g, ffn_w_gate, ffn_w_up, ffn_w_down,
           mla_w_in, mla_q_gain, mla_kv_gain, mla_w_uq, mla_w_ukv, mla_w_o,
           gqa_w_in, gqa_q_gain, gqa_k_gain, gqa_w_o,
           na_w_in, na_rpb, na_w_o,
           ret_w_in, ret_decay_log2, ret_w_o):
    b, s, d = x.shape
    tcx = ctx.shape[1]
    depth = mod_w.shape[0]
    assert depth == 4 and s % (NA_QROWS * GRID_W) == 0 and s // GRID_W >= NA_QROWS + NA_WIN_R
    heads = d // HEAD_DIM

    cc = jnp.concatenate([c, c_ctx[None], jnp.zeros((8 - b - 1, d), F32)], axis=0)
    mods = _mod_vectors(cc, mod_w, mod_b).reshape(depth, 8, N_MOD, d)

    h = x.reshape(b * s, d)
    hc = ctx.reshape(b * tcx, d)

    for i in range(depth):
        kind = i % 4
        last = i == depth - 1
        ml = [mods[i, :b, j][:, None, :] for j in range(N_MOD)]
        mc = [mods[i, b:b + 1, j][:, None, :] for j in range(N_MOD)]
        gn = [norm_g[i, j][None, :] for j in range(6)]

        def ffn(hh, m, f):
            return _ffn(hh, m[3 * f], m[3 * f + 1], m[3 * f + 2], gn[2 * f], gn[2 * f + 1],
                        ffn_w_gate[i, f // 2].astype(BF16), ffn_w_up[i, f // 2].astype(BF16),
                        ffn_w_down[i, f // 2].astype(BF16))

        h = ffn(h, ml, 0)
        hc = ffn(hc, mc, 0)

        def in_proj(w, out_dtype):
            zl = _proj(h, gn[2], w, shift=ml[3], scale=ml[4], out_dtype=out_dtype)
            zc = _proj(hc, gn[2], w, shift=mc[3], scale=mc[4], out_dtype=out_dtype)
            return zl, zc

        if kind == 0:
            y, yc = _mla_mixer(in_proj, b, s, tcx, heads, mla_w_in[0], mla_q_gain[0], mla_kv_gain[0],
                               mla_w_uq[0], mla_w_ukv[0])
            w_o = mla_w_o[0]
        elif kind == 1:
            y, yc = _gqa_mixer(in_proj, b, s, tcx, heads, gqa_w_in[0], gqa_q_gain[0], gqa_k_gain[0])
            w_o = gqa_w_o[0]
        elif kind == 2:
            y, yc = _na_mixer(in_proj, b, s, tcx, heads, na_w_in[0], na_rpb[0])
            w_o = na_w_o[0]
        else:
            y, yc = _ret_mixer(in_proj, b, s, tcx, ret_w_in[0], ret_decay_log2[0]), None
            w_o = ret_w_o[0]

        w_o = w_o.astype(BF16)
        h = _out_proj(y.reshape(b * s, -1), w_o, gn[3], ml[5], h)
        h = ffn(h, ml, 2)
        if not last:
            hc = _out_proj(yc.reshape(b * tcx, -1), w_o, gn[3], mc[5], hc)
            hc = ffn(hc, mc, 2)
    return h.reshape(b, s, d)
```

```python
import functools

import jax
import jax.numpy as jnp
from jax import lax
from jax.experimental import pallas as pl
from jax.experimental.pallas import tpu as pltpu

F32 = jnp.float32
BF16 = jnp.bfloat16

RMS_EPS = 1e-6
ROPE_BASE = 10000.0
FFN_RES = 0.5
GRID_W = 64
N_MOD = 9
LANES = 128
ROW_CHUNK = 64
APPLY_CHUNK = 32
HEAD_DIM = 128
MLA_ROPE = 64
RET_QK = 256
RET_V = 512
RET_CHUNK = 512
NA_WIN_R = 8
NA_WIN_C = 16
NA_QROWS = 8
NEG_INF = -1e30
LOG2_E = 1.4426950408889634
VMEM_LIMIT_BYTES = 56 * 1024 * 1024


def _cparams(semantics):
    return pltpu.CompilerParams(dimension_semantics=semantics, vmem_limit_bytes=VMEM_LIMIT_BYTES)


def _tile(n, pref):
    if n <= pref:
        return n
    t = pref
    while n % t:
        t //= 2
    return t


def _rms(x):
    return x * lax.rsqrt(jnp.mean(x * x, axis=-1, keepdims=True) + RMS_EPS)


def _silu(x):
    return x * jax.nn.sigmoid(x)


def _row_chunks(n_rows):
    step = min(ROW_CHUNK, n_rows)
    return [pl.ds(r, step) for r in range(0, n_rows, step)]


def _row_rsqrt(x_ref, r_ref):
    for rows in _row_chunks(x_ref.shape[0]):
        x = x_ref[rows, :].astype(F32)
        ms = jnp.mean(x * x, axis=-1, keepdims=True)
        r_ref[rows, :] = jnp.broadcast_to(lax.rsqrt(ms + RMS_EPS), (rows.size, LANES))


def _loop_row_chunks(n_rows, body):
    step = min(APPLY_CHUNK, n_rows)

    def it(c, carry):
        body(pl.ds(pl.multiple_of(c * step, step), step))
        return carry

    lax.fori_loop(0, n_rows // step, it, 0, unroll=2)


def _norm_modulate_rows(x_ref, u_ref, r_ref, gain, shift):
    _row_rsqrt(x_ref, r_ref)

    def apply(rows):
        u = x_ref[rows, :].astype(F32) * _lane_tile(r_ref[rows, :], x_ref.shape[1] // LANES) * gain
        u_ref[rows, :] = (u if shift is None else u + shift).astype(u_ref.dtype)

    _loop_row_chunks(x_ref.shape[0], apply)


def _norm_residual_rows(y_ref, h_ref, o_ref, r_ref, gain):
    _row_rsqrt(y_ref, r_ref)

    def apply(rows):
        y = y_ref[rows, :] * _lane_tile(r_ref[rows, :], y_ref.shape[1] // LANES)
        o_ref[rows, :] = h_ref[rows, :] + y * gain

    _loop_row_chunks(y_ref.shape[0], apply)


def _lane_tile(x, n):
    return x if n == 1 else jnp.concatenate([x] * n, axis=1)


def _mod_kernel(cc_ref, w_ref, b_ref, o_ref):
    s = _silu(cc_ref[...]).astype(BF16)
    o_ref[0] = jnp.dot(s, w_ref[0].astype(BF16), preferred_element_type=F32) + b_ref[0]


def _mod_vectors(cc, mod_w, mod_b):
    depth, d, n = mod_w.shape
    rows = cc.shape[0]
    tn = _tile(n, 1024)
    return pl.pallas_call(
        _mod_kernel,
        grid=(depth, n // tn),
        in_specs=[
            pl.BlockSpec((rows, d), lambda l, j: (0, 0)),
            pl.BlockSpec((1, d, tn), lambda l, j: (l, 0, j)),
            pl.BlockSpec((1, 1, tn), lambda l, j: (l, 0, j)),
        ],
        out_specs=pl.BlockSpec((1, rows, tn), lambda l, j: (l, 0, j)),
        out_shape=jax.ShapeDtypeStruct((depth, rows, n), F32),
        compiler_params=_cparams(("arbitrary", "arbitrary")),
        name="mod_vectors",
    )(cc, mod_w, mod_b.reshape(depth, 1, n))


def _ffn_kernel(h_ref, sh_ref, sc_ref, gt_ref, gpre_ref, gpost_ref, wg_ref, wu_ref, wd_ref,
                o_ref, u_ref, acc_ref, r_ref, *, nk):
    k = pl.program_id(1)

    @pl.when(k == 0)
    def _():
        _norm_modulate_rows(h_ref, u_ref, r_ref, gpre_ref[...] * (1.0 + sc_ref[0]), sh_ref[0])
        acc_ref[...] = jnp.zeros_like(acc_ref)

    u = u_ref[...]
    g = jnp.dot(u, wg_ref[...], preferred_element_type=F32)
    up = jnp.dot(u, wu_ref[...], preferred_element_type=F32)
    a = (_silu(g) * up).astype(BF16)
    acc_ref[...] += jnp.dot(a, wd_ref[...], preferred_element_type=F32)

    @pl.when(k == nk - 1)
    def _():
        _norm_residual_rows(acc_ref, h_ref, o_ref, r_ref, FFN_RES * gt_ref[0] * gpost_ref[...])


def _ffn(h, shift, scale, gate, g_pre, g_post, wg, wu, wd, *, tm=512, tf=512):
    m, d = h.shape
    f = wg.shape[1]
    bm = shift.shape[0]
    rows_per_b = m // bm
    tm = _tile(rows_per_b, tm)
    tf = _tile(f, tf)
    nk = f // tf
    per_b = rows_per_b // tm
    mod_spec = pl.BlockSpec((1, 1, d), lambda i, k: (i // per_b, 0, 0))
    vec_spec = pl.BlockSpec((1, d), lambda i, k: (0, 0))
    return pl.pallas_call(
        functools.partial(_ffn_kernel, nk=nk),
        grid=(m // tm, nk),
        in_specs=[
            pl.BlockSpec((tm, d), lambda i, k: (i, 0)),
            mod_spec, mod_spec, mod_spec, vec_spec, vec_spec,
            pl.BlockSpec((d, tf), lambda i, k: (0, k)),
            pl.BlockSpec((d, tf), lambda i, k: (0, k)),
            pl.BlockSpec((tf, d), lambda i, k: (k, 0)),
        ],
        out_specs=pl.BlockSpec((tm, d), lambda i, k: (i, 0)),
        out_shape=jax.ShapeDtypeStruct((m, d), F32),
        scratch_shapes=[pltpu.VMEM((tm, d), BF16), pltpu.VMEM((tm, d), F32), pltpu.VMEM((tm, LANES), F32)],
        compiler_params=_cparams(("parallel", "arbitrary")),
        name="ffn_half_step",
    )(h, shift, scale, gate, g_pre, g_post, wg, wu, wd)


def _proj_kernel(*refs, modulated):
    if modulated:
        x_ref, g_ref, sh_ref, sc_ref, w_ref, o_ref, u_ref, r_ref = refs
    else:
        x_ref, g_ref, w_ref, o_ref, u_ref, r_ref = refs

    @pl.when(pl.program_id(1) == 0)
    def _():
        if modulated:
            _norm_modulate_rows(x_ref, u_ref, r_ref, g_ref[...] * (1.0 + sc_ref[0]), sh_ref[0])
        else:
            _norm_modulate_rows(x_ref, u_ref, r_ref, g_ref[...], None)

    o_ref[...] = jnp.dot(u_ref[...], w_ref[...], preferred_element_type=F32).astype(o_ref.dtype)


def _proj(x, gain, w, *, out_dtype, shift=None, scale=None, kblock=0, tm=512):
    m = x.shape[0]
    k, n = w.shape
    modulated = shift is not None
    bm = shift.shape[0] if modulated else 1
    rows_per_b = m // bm
    tm = _tile(rows_per_b, tm)
    per_b = rows_per_b // tm
    tn = n if k * n * 2 <= 6 * 1024 * 1024 else _tile(n, 1024)
    in_specs = [pl.BlockSpec((tm, k), lambda i, j: (i, kblock)),
                pl.BlockSpec((1, k), lambda i, j: (0, 0))]
    args = [x, gain]
    if modulated:
        mod_spec = pl.BlockSpec((1, 1, k), lambda i, j: (i // per_b, 0, 0))
        in_specs += [mod_spec, mod_spec]
        args += [shift, scale]
    in_specs.append(pl.BlockSpec((k, tn), lambda i, j: (0, j)))
    args.append(w)
    return pl.pallas_call(
        functools.partial(_proj_kernel, modulated=modulated),
        grid=(m // tm, n // tn),
        in_specs=in_specs,
        out_specs=pl.BlockSpec((tm, tn), lambda i, j: (i, j)),
        out_shape=jax.ShapeDtypeStruct((m, n), out_dtype),
        scratch_shapes=[pltpu.VMEM((tm, k), BF16), pltpu.VMEM((tm, LANES), F32)],
        compiler_params=_cparams(("parallel", "arbitrary")),
        name="norm_proj",
    )(*args)


def _out_proj_kernel(a_ref, w_ref, gpost_ref, gt_ref, h_ref, o_ref, acc_ref, r_ref, *, nk):
    if nk == 1:
        y = jnp.dot(a_ref[...], w_ref[...], preferred_element_type=F32)
        o_ref[...] = h_ref[...] + _rms(y) * (gt_ref[0] * gpost_ref[...])
        return
    k = pl.program_id(1)

    @pl.when(k == 0)
    def _():
        acc_ref[...] = jnp.zeros_like(acc_ref)

    acc_ref[...] += jnp.dot(a_ref[...], w_ref[...], preferred_element_type=F32)

    @pl.when(k == nk - 1)
    def _():
        _norm_residual_rows(acc_ref, h_ref, o_ref, r_ref, gt_ref[0] * gpost_ref[...])


def _out_proj(a, w, g_post, gate, h, *, tm=512, tk=2048):
    m, kdim = a.shape
    d = w.shape[1]
    bm = gate.shape[0]
    rows_per_b = m // bm
    tm = _tile(rows_per_b, tm)
    per_b = rows_per_b // tm
    tk = _tile(kdim, tk)
    nk = kdim // tk
    return pl.pallas_call(
        functools.partial(_out_proj_kernel, nk=nk),
        grid=(m // tm, nk),
        in_specs=[
            pl.BlockSpec((tm, tk), lambda i, k: (i, k)),
            pl.BlockSpec((tk, d), lambda i, k: (k, 0)),
            pl.BlockSpec((1, d), lambda i, k: (0, 0)),
            pl.BlockSpec((1, 1, d), lambda i, k: (i // per_b, 0, 0)),
            pl.BlockSpec((tm, d), lambda i, k: (i, 0)),
        ],
        out_specs=pl.BlockSpec((tm, d), lambda i, k: (i, 0)),
        out_shape=jax.ShapeDtypeStruct((m, d), F32),
        scratch_shapes=[pltpu.VMEM((tm, d), F32), pltpu.VMEM((tm, LANES), F32)],
        compiler_params=_cparams(("parallel", "arbitrary")),
        name="out_proj",
    )(a, w, g_post, gate, h)


def _rope128(x, c_ref, s_ref):
    return x * c_ref[...] + pltpu.roll(x, 64, axis=1) * s_ref[...]


def _prep_kernel(*refs, mode, rope, norm, scale, heads, xcol, x2col):
    refs = list(refs)
    o_ref = refs.pop()
    x_ref = refs.pop(0)
    x2_ref = refs.pop(0) if mode == "mla_k" else None
    g_ref = refs.pop(0) if norm else None
    c_ref, s_ref = (refs.pop(0), refs.pop(0)) if rope else (None, None)
    wx = HEAD_DIM if mode in ("head", "mla_k") else 2 * HEAD_DIM
    wo = HEAD_DIM if mode == "head" else 2 * HEAD_DIM

    if mode == "mla_k":
        shared = x2_ref[0, :, pl.ds(x2col * LANES, LANES)].astype(F32)
        if rope:
            shared = _rope128(shared, c_ref, s_ref)
        shared = shared.astype(o_ref.dtype)
    for h in range(heads):
        x = x_ref[0, :, pl.ds(xcol(h) * wx, wx)].astype(F32)
        if mode == "head":
            if norm:
                x = _rms(x) * g_ref[...]
            if rope:
                x = _rope128(x, c_ref, s_ref)
            o_ref[0, :, pl.ds(h * wo, wo)] = (x * scale).astype(o_ref.dtype)
        elif mode == "mla_q":
            r = x[:, HEAD_DIM:]
            if rope:
                r = _rope128(r, c_ref, s_ref)
            o_ref[0, :, pl.ds(h * wo, HEAD_DIM)] = (x[:, :HEAD_DIM] * scale).astype(o_ref.dtype)
            o_ref[0, :, pl.ds(h * wo + HEAD_DIM, HEAD_DIM)] = (r * scale).astype(o_ref.dtype)
        else:
            o_ref[0, :, pl.ds(h * wo, HEAD_DIM)] = x.astype(o_ref.dtype)
            o_ref[0, :, pl.ds(h * wo + HEAD_DIM, HEAD_DIM)] = shared


def _prep(x, *, mode, heads, xcol, x2=None, x2col=None, gain=None, tables=None, scale=1.0, ts=256):
    b, t, _ = x.shape
    ts = _tile(t, ts)
    wo = HEAD_DIM if mode == "head" else 2 * HEAD_DIM
    in_specs = [pl.BlockSpec((1, ts, x.shape[2]), lambda bb, i: (bb, i, 0))]
    args = [x]
    if mode == "mla_k":
        in_specs.append(pl.BlockSpec((1, ts, x2.shape[2]), lambda bb, i: (bb, i, 0)))
        args.append(x2)
    if gain is not None:
        in_specs.append(pl.BlockSpec((1, HEAD_DIM), lambda bb, i: (0, 0)))
        args.append(gain)
    if tables is not None:
        tspec = pl.BlockSpec((ts, LANES), lambda bb, i: (i, 0))
        in_specs += [tspec, tspec]
        args += list(tables)
    return pl.pallas_call(
        functools.partial(_prep_kernel, mode=mode, rope=tables is not None, norm=gain is not None,
                          scale=scale, heads=heads, xcol=xcol, x2col=x2col),
        grid=(b, t // ts),
        in_specs=in_specs,
        out_specs=pl.BlockSpec((1, ts, heads * wo), lambda bb, i: (bb, i, 0)),
        out_shape=jax.ShapeDtypeStruct((b, t, heads * wo), BF16),
        compiler_params=_cparams(("parallel", "parallel")),
        name="head_prep_" + mode,
    )(*args)


def _flash_kernel(*refs, nk, scale, has_lat, n_sub):
    if has_lat:
        q_ref, kc_ref, vc_ref, kl_ref, vl_ref, o_ref, m_ref, acc_ref, va_ref = refs
    else:
        q_ref, kc_ref, vc_ref, o_ref, m_ref, acc_ref, va_ref = refs
    kk = pl.program_id(3)
    tq = q_ref.shape[1]
    dv = o_ref.shape[2]
    tr = tq // n_sub

    def step(k, v):
        tk = k.shape[0]
        va_ref[pl.ds(0, tk), pl.ds(0, dv)] = v
        va = va_ref[pl.ds(0, tk), :]

        def scores(r):
            q = q_ref[0, pl.ds(r * tr, tr), :]
            if scale != 1.0:
                q = (q.astype(F32) * scale).astype(BF16)
            return lax.dot_general(q, k, (((1,), (1,)), ((), ())), preferred_element_type=F32)

        s_next = scores(0)
        for r in range(n_sub):
            rows = pl.ds(r * tr, tr)
            s = s_next
            if r + 1 < n_sub:
                s_next = scores(r + 1)
            m_prev = m_ref[rows, :]
            m_new = jnp.maximum(m_prev, jnp.max(s, axis=-1, keepdims=True))
            alpha = jnp.exp2(m_prev - m_new)
            p = jnp.exp2((s - _lane_tile(m_new, tk // LANES)).astype(BF16))
            acc_ref[rows, :] = (_lane_tile(alpha, 2 * dv // LANES) * acc_ref[rows, :]
                                + jnp.dot(p, va, preferred_element_type=F32))
            m_ref[rows, :] = m_new

    @pl.when(kk == 0)
    def _():
        m_ref[...] = jnp.full_like(m_ref, NEG_INF)
        acc_ref[...] = jnp.zeros_like(acc_ref)
        va_ref[:, pl.ds(dv, dv)] = jnp.ones((va_ref.shape[0], dv), va_ref.dtype)
        step(kc_ref[0], vc_ref[0])

    if has_lat:
        @pl.when(kk > 0)
        def _():
            step(kl_ref[0], vl_ref[0])

    @pl.when(kk == nk - 1)
    def _():
        o_ref[0] = (acc_ref[:, pl.ds(0, dv)] / acc_ref[:, pl.ds(dv, dv)]).astype(o_ref.dtype)


def _flash(q, kc, vc, kl=None, vl=None, *, heads, dq, dv, qcol, kcol, vcol, scale=1.0,
           tq=4096, tk=2048, tr=512):
    b, s, _ = q.shape
    tc = kc.shape[1]
    tq = _tile(s, tq)
    has_lat = kl is not None
    if has_lat:
        tk = _tile(kl.shape[1], tk)
        nk = 1 + kl.shape[1] // tk
    else:
        nk = 1
    in_specs = [
        pl.BlockSpec((1, tq, dq), lambda bb, h, i, kk: (bb, i, qcol(h))),
        pl.BlockSpec((1, tc, dq), lambda bb, h, i, kk: (bb, 0, kcol(h))),
        pl.BlockSpec((1, tc, dv), lambda bb, h, i, kk: (bb, 0, vcol(h))),
    ]
    args = [q, kc, vc]
    if has_lat:
        in_specs += [
            pl.BlockSpec((1, tk, dq), lambda bb, h, i, kk: (bb, jnp.maximum(kk - 1, 0), kcol(h))),
            pl.BlockSpec((1, tk, dv), lambda bb, h, i, kk: (bb, jnp.maximum(kk - 1, 0), vcol(h))),
        ]
        args += [kl, vl]
    return pl.pallas_call(
        functools.partial(_flash_kernel, nk=nk, scale=scale, has_lat=has_lat, n_sub=tq // _tile(tq, tr)),
        grid=(b, heads, s // tq, nk),
        in_specs=in_specs,
        out_specs=pl.BlockSpec((1, tq, dv), lambda bb, h, i, kk: (bb, i, h)),
        out_shape=jax.ShapeDtypeStruct((b, s, heads * dv), BF16),
        scratch_shapes=[pltpu.VMEM((tq, LANES), F32), pltpu.VMEM((tq, 2 * dv), F32),
                        pltpu.VMEM((max(tc, tk), 2 * dv), BF16)],
        compiler_params=_cparams(("parallel", "parallel", "parallel", "arbitrary")),
        name="flash_attention",
    )(*args)


def _na_kernel(q_ref, k_ref, v_ref, kc_ref, vc_ref, bias_ref, o_ref, *, band, seq, scale, n_sub):
    i = pl.program_id(2)
    tq = q_ref.shape[1]
    start = jnp.clip(i * tq - (band - tq) // 2, 0, seq - band)
    start = pl.multiple_of(start, 256)
    dv = o_ref.shape[2]
    kb = k_ref[0, pl.ds(start, band), :]
    kc = kc_ref[0]
    vb = jnp.concatenate([v_ref[0, pl.ds(start, band), :], jnp.ones((band, dv), BF16)], axis=1)
    vc = jnp.concatenate([vc_ref[0], jnp.ones((kc.shape[0], dv), BF16)], axis=1)
    dn = (((1,), (1,)), ((), ()))
    tr = tq // n_sub
    for r in range(n_sub):
        rows = pl.ds(r * tr, tr)
        q = (q_ref[0, rows, :].astype(F32) * scale).astype(BF16)
        s_nb = lax.dot_general(q, kb, dn, preferred_element_type=F32) + bias_ref[0, 0, rows, :]
        s_cx = lax.dot_general(q, kc, dn, preferred_element_type=F32)
        m = jnp.maximum(jnp.max(s_nb, axis=-1, keepdims=True), jnp.max(s_cx, axis=-1, keepdims=True))
        p_nb = jnp.exp2((s_nb - m).astype(BF16))
        p_cx = jnp.exp2((s_cx - m).astype(BF16))
        o = (jnp.dot(p_nb, vb, preferred_element_type=F32) + jnp.dot(p_cx, vc, preferred_element_type=F32))
        o_ref[0, rows, :] = (o[:, :dv] / o[:, dv:]).astype(o_ref.dtype)


def _na_bias_tiles(rpb, rows):
    nr, wr, wc, w = NA_QROWS, NA_WIN_R, NA_WIN_C, GRID_W
    brows = nr + wr
    heads = rpb.shape[0]
    lpad = w - wc
    vp = jnp.pad(rpb.astype(F32) * LOG2_E, ((0, 0), (0, 0), (lpad, lpad)), constant_values=NEG_INF)
    toep = jnp.stack([vp[:, :, w - 1 - c: 2 * w - 1 - c] for c in range(w)], axis=2)
    col = jnp.arange(w, dtype=jnp.int32)
    c0 = jnp.clip(col - wc // 2, 0, w - wc)
    in_col = (col[None, :] >= c0[:, None]) & (col[None, :] < c0[:, None] + wc)
    toep = jnp.where(in_col[None, None], toep, NEG_INF)
    masked = jnp.full((heads, 1, w, w), NEG_INF, F32)
    tiles = []
    for r_first, b_first in ((0, 0), (nr, nr - wr // 2), (rows - nr, rows - brows)):
        per_row = []
        for a in range(nr):
            r = r_first + a
            r0 = min(max(r - wr // 2, 0), rows - wr)
            lo = r0 - b_first
            dr_lo = r0 - r + (wr - 1)
            blk = toep[:, dr_lo: dr_lo + wr]
            blk = jnp.concatenate([jnp.tile(masked, (1, lo, 1, 1)), blk,
                                   jnp.tile(masked, (1, brows - wr - lo, 1, 1))], axis=1)
            per_row.append(jnp.swapaxes(blk, 1, 2))
        tiles.append(jnp.stack(per_row, axis=1).reshape(heads, nr * w, brows * w))
    return jnp.stack(tiles, axis=1)


def _na_attention(z, zc, rpb, *, heads):
    b, s, _ = z.shape
    tcx = zc.shape[1]
    rows = s // GRID_W
    tq = NA_QROWS * GRID_W
    band = (NA_QROWS + NA_WIN_R) * GRID_W
    nblk = s // tq
    bias = _na_bias_tiles(rpb, rows)
    scale = HEAD_DIM ** -0.5 * LOG2_E

    def variant(i):
        return jnp.where(i == 0, 0, jnp.where(i == nblk - 1, 2, 1))

    return pl.pallas_call(
        functools.partial(_na_kernel, band=band, seq=s, scale=scale, n_sub=4),
        grid=(b, heads, nblk),
        in_specs=[
            pl.BlockSpec((1, tq, HEAD_DIM), lambda bb, h, i: (bb, i, h)),
            pl.BlockSpec((1, s, HEAD_DIM), lambda bb, h, i: (bb, 0, heads + h)),
            pl.BlockSpec((1, s, HEAD_DIM), lambda bb, h, i: (bb, 0, 2 * heads + h)),
            pl.BlockSpec((1, tcx, HEAD_DIM), lambda bb, h, i: (bb, 0, heads + h)),
            pl.BlockSpec((1, tcx, HEAD_DIM), lambda bb, h, i: (bb, 0, 2 * heads + h)),
            pl.BlockSpec((1, 1, tq, band), lambda bb, h, i: (h, variant(i), 0, 0)),
        ],
        out_specs=pl.BlockSpec((1, tq, HEAD_DIM), lambda bb, h, i: (bb, i, h)),
        out_shape=jax.ShapeDtypeStruct((b, s, heads * HEAD_DIM), BF16),
        compiler_params=_cparams(("parallel", "parallel", "arbitrary")),
        name="neighbourhood_attention",
    )(z, z, z, zc, zc, bias)


def _ret_kernel(dec_ref, ql_ref, kl_ref, vl_ref, kc_ref, vc_ref, cos_ref, sin_ref, o_ref, st_ref, din_ref,
                *, nc):
    h = pl.program_id(1)
    d = pl.program_id(2)
    t = pl.program_id(3)
    half = RET_QK // 2
    lg = jnp.log1p(-jnp.exp2(jnp.full((1, 1), dec_ref[d, h], F32)))
    fwd = d == 0

    def rope(x):
        c, s = cos_ref[...], sin_ref[...]
        x1, x2 = x[:, :half], x[:, half:]
        return jnp.concatenate([x1 * c - x2 * s, x2 * c + x1 * s], axis=-1)

    def update(k, v):
        n = k.shape[0]
        pos = lax.broadcasted_iota(jnp.int32, (n, 1), 0).astype(F32)
        zeta = jnp.exp(jnp.where(fwd, n - 1.0 - pos, pos) * lg)
        kz = (k * zeta).astype(BF16)
        kv = lax.dot_general(kz, v, (((0,), (0,)), ((), ())), preferred_element_type=F32)
        st_ref[...] = st_ref[...] * jnp.exp(n * lg) + kv

    @pl.when(t == 0)
    def _():
        st_ref[...] = jnp.zeros_like(st_ref)
        n = din_ref.shape[0]
        ri = lax.broadcasted_iota(jnp.int32, (n, n), 0)
        ci = lax.broadcasted_iota(jnp.int32, (n, n), 1)
        dist = jnp.where(fwd, ri - ci, ci - ri)
        din_ref[...] = jnp.where(dist >= 0, jnp.exp(jnp.maximum(dist, 0).astype(F32) * lg), 0.0)

    @pl.when(t < nc)
    def _():
        update(kc_ref[0].astype(F32) * RET_QK ** -0.5, vc_ref[0])

    @pl.when(t >= nc)
    def _():
        q = rope(ql_ref[0].astype(F32))
        k = rope(kl_ref[0].astype(F32) * RET_QK ** -0.5)
        v = vl_ref[0]
        n = q.shape[0]
        pos = lax.broadcasted_iota(jnp.int32, (n, 1), 0).astype(F32)
        qb = q.astype(BF16)
        s = lax.dot_general(qb, k.astype(BF16), (((1,), (1,)), ((), ())),
                            preferred_element_type=F32) * din_ref[...]
        xi = jnp.exp(jnp.where(fwd, pos + 1.0, n - pos) * lg)
        o = (jnp.dot(s.astype(BF16), v, preferred_element_type=F32)
             + jnp.dot(qb, st_ref[...].astype(BF16), preferred_element_type=F32) * xi)
        o_ref[0, 0] = o.astype(o_ref.dtype)
        update(k, v)


def _retention(z, zc, decay_log2, cos, sin, *, heads):
    b, s, _ = z.shape
    tcx = zc.shape[1]
    chunk = _tile(s, RET_CHUNK)
    cchunk = _tile(tcx, RET_CHUNK)
    nc, nl = tcx // cchunk, s // chunk
    kcol0 = heads
    vcol0 = (2 * heads * RET_QK) // RET_V

    def cidx(d, t):
        c = jnp.minimum(t, nc - 1)
        return jnp.where(d == 0, c, nc - 1 - c)

    def lidx(d, t):
        c = jnp.maximum(t - nc, 0)
        return jnp.where(d == 0, c, nl - 1 - c)

    return pl.pallas_call(
        functools.partial(_ret_kernel, nc=nc),
        grid=(b, heads, 2, nc + nl),
        in_specs=[
            pl.BlockSpec(memory_space=pltpu.SMEM),
            pl.BlockSpec((1, chunk, RET_QK), lambda bb, h, d, t: (bb, lidx(d, t), h)),
            pl.BlockSpec((1, chunk, RET_QK), lambda bb, h, d, t: (bb, lidx(d, t), kcol0 + h)),
            pl.BlockSpec((1, chunk, RET_V), lambda bb, h, d, t: (bb, lidx(d, t), vcol0 + h)),
            pl.BlockSpec((1, cchunk, RET_QK), lambda bb, h, d, t: (bb, cidx(d, t), kcol0 + h)),
            pl.BlockSpec((1, cchunk, RET_V), lambda bb, h, d, t: (bb, cidx(d, t), vcol0 + h)),
            pl.BlockSpec((chunk, RET_QK // 2), lambda bb, h, d, t: (lidx(d, t), 0)),
            pl.BlockSpec((chunk, RET_QK // 2), lambda bb, h, d, t: (lidx(d, t), 0)),
        ],
        out_specs=pl.BlockSpec((1, 1, chunk, RET_V), lambda bb, h, d, t: (d, bb, lidx(d, t), h)),
        out_shape=jax.ShapeDtypeStruct((2, b, s, heads * RET_V), BF16),
        scratch_shapes=[pltpu.VMEM((RET_QK, RET_V), F32), pltpu.VMEM((chunk, chunk), F32)],
        compiler_params=_cparams(("parallel", "parallel", "arbitrary", "arbitrary")),
        name="retention_scan",
    )(decay_log2, z, z, z, zc, zc, cos, sin)


def _ret_gate_kernel(of_ref, ob_ref, g_ref, y_ref):
    o = of_ref[0, 0].astype(F32) + ob_ref[0, 0].astype(F32)
    y_ref[0] = (_silu(g_ref[0].astype(F32)) * _rms(o)).astype(y_ref.dtype)


def _ret_gate(o2, z, *, heads, ts=512):
    _, b, s, _ = o2.shape
    ts = _tile(s, ts)
    gcol0 = (2 * heads * RET_QK + heads * RET_V) // RET_V
    return pl.pallas_call(
        _ret_gate_kernel,
        grid=(b, s // ts, heads),
        in_specs=[
            pl.BlockSpec((1, 1, ts, RET_V), lambda bb, i, h: (0, bb, i, h)),
            pl.BlockSpec((1, 1, ts, RET_V), lambda bb, i, h: (1, bb, i, h)),
            pl.BlockSpec((1, ts, RET_V), lambda bb, i, h: (bb, i, gcol0 + h)),
        ],
        out_specs=pl.BlockSpec((1, ts, RET_V), lambda bb, i, h: (bb, i, h)),
        out_shape=jax.ShapeDtypeStruct((b, s, heads * RET_V), BF16),
        compiler_params=_cparams(("parallel", "parallel", "arbitrary")),
        name="retention_gate",
    )(o2, o2, z)


def _axial_angles(n_tokens, rot_dim):
    t = jnp.arange(n_tokens, dtype=jnp.int32)
    row = (t // GRID_W).astype(F32)
    col = (t % GRID_W).astype(F32)
    n_f = rot_dim // 4
    freqs = ROPE_BASE ** (-jnp.arange(n_f, dtype=F32) / n_f)
    return jnp.concatenate([row[:, None] * freqs, col[:, None] * freqs], axis=-1)


def _rope_slot_tables(ang):
    pad = jnp.zeros((ang.shape[0], 64 - ang.shape[1]), F32)
    cos, sin = jnp.cos(ang), jnp.sin(ang)
    return (jnp.concatenate([cos, pad, cos, pad], axis=-1),
            jnp.concatenate([-sin, pad, sin, pad], axis=-1))


def _spread_rope_cols(w):
    half = MLA_ROPE // 2
    z = jnp.zeros(w.shape[:-1] + (64 - half,), w.dtype)
    return jnp.concatenate([w[..., :half], z, w[..., half:], z], axis=-1)


def _mla_mixer(in_proj, b, s, tcx, heads, w_in, q_gain, kv_gain, w_uq, w_ukv):
    rq = q_gain.shape[-1]
    w_in = jnp.concatenate([w_in[:, :2 * rq], _spread_rope_cols(w_in[:, 2 * rq:])], axis=-1).astype(BF16)
    w_uq = w_uq.reshape(rq, heads, HEAD_DIM + MLA_ROPE)
    w_uq = jnp.concatenate([w_uq[..., :HEAD_DIM], _spread_rope_cols(w_uq[..., HEAD_DIM:])],
                           axis=-1).reshape(rq, heads * 2 * HEAD_DIM).astype(BF16)
    w_ukv = w_ukv.astype(BF16)
    tables = _rope_slot_tables(_axial_angles(s, MLA_ROPE))
    scale = (HEAD_DIM + MLA_ROPE) ** -0.5 * LOG2_E
    zl, zc = in_proj(w_in, F32)
    ropecol = 2 * rq // LANES

    def qkv(z, n_tok, tabs):
        q = _proj(z, q_gain[None], w_uq, kblock=0, out_dtype=BF16).reshape(b, n_tok, -1)
        kv = _proj(z, kv_gain[None], w_ukv, kblock=1, out_dtype=BF16).reshape(b, n_tok, -1)
        q = _prep(q, mode="mla_q", heads=heads, xcol=lambda hh: hh, tables=tabs, scale=scale)
        k = _prep(kv, mode="mla_k", heads=heads, xcol=lambda hh: 2 * hh,
                  x2=z.reshape(b, n_tok, -1), x2col=ropecol, tables=tabs)
        return q, k, kv

    ql, kl, kvl = qkv(zl, s, tables)
    qc, kc, kvc = qkv(zc, tcx, None)
    cols = dict(heads=heads, dq=2 * HEAD_DIM, dv=HEAD_DIM, qcol=lambda hh: hh, kcol=lambda hh: hh,
                vcol=lambda hh: 2 * hh + 1)
    return _flash(ql, kc, kvc, kl, kvl, **cols), _flash(qc, kc, kvc, **cols)


def _gqa_mixer(in_proj, b, s, tcx, heads, w_in, q_gain, k_gain):
    hk = (w_in.shape[-1] // HEAD_DIM - heads) // 2
    grp = heads // hk
    tables = _rope_slot_tables(_axial_angles(s, HEAD_DIM))
    zl, zc = in_proj(w_in.astype(BF16), BF16)
    zl = zl.reshape(b, s, -1)
    zc = zc.reshape(b, tcx, -1)
    scale = HEAD_DIM ** -0.5 * LOG2_E
    qg, kg = q_gain[None], k_gain[None]
    ql = _prep(zl, mode="head", heads=heads, xcol=lambda hh: hh, gain=qg, tables=tables, scale=scale)
    kl = _prep(zl, mode="head", heads=hk, xcol=lambda hh: heads + hh, gain=kg, tables=tables)
    qc = _prep(zc, mode="head", heads=heads, xcol=lambda hh: hh, gain=qg, scale=scale)
    kc = _prep(zc, mode="head", heads=hk, xcol=lambda hh: heads + hh, gain=kg)
    cols = dict(heads=heads, dq=HEAD_DIM, dv=HEAD_DIM, qcol=lambda hh: hh,
                kcol=lambda hh: hh // grp, vcol=lambda hh: heads + hk + hh // grp)
    return _flash(ql, kc, zc, kl, zl, **cols), _flash(qc, kc, zc, **cols)


def _na_mixer(in_proj, b, s, tcx, heads, w_in, rpb):
    zl, zc = in_proj(w_in.astype(BF16), BF16)
    zl = zl.reshape(b, s, -1)
    zc = zc.reshape(b, tcx, -1)
    y = _na_attention(zl, zc, rpb, heads=heads)
    yc = _flash(zc, zc, zc, heads=heads, dq=HEAD_DIM, dv=HEAD_DIM, qcol=lambda hh: hh,
                kcol=lambda hh: heads + hh, vcol=lambda hh: 2 * heads + hh,
                scale=HEAD_DIM ** -0.5 * LOG2_E)
    return y, yc


def _ret_mixer(in_proj, b, s, tcx, w_in, decay_log2):
    rh = decay_log2.shape[-1]
    pos = jnp.arange(s, dtype=F32)
    freqs = ROPE_BASE ** (-jnp.arange(RET_QK // 2, dtype=F32) / (RET_QK // 2))
    ang = pos[:, None] * freqs
    zl, zc = in_proj(w_in.astype(BF16), BF16)
    zl = zl.reshape(b, s, -1)
    zc = zc.reshape(b, tcx, -1)
    o2 = _retention(zl, zc, decay_log2.astype(F32), jnp.cos(ang), jnp.sin(ang), heads=rh)
    return _ret_gate(o2, zl, heads=rh)


def kernel(x, c, ctx, c_ctx, mod_w, mod_b, norm_g, ffn_w_gate, ffn_w_up, ffn_w_down,
           mla_w_in, mla_q_gain, mla_kv_gain, mla_w_uq, mla_w_ukv, mla_w_o,
           gqa_w_in, gqa_q_gain, gqa_k_gain, gqa_w_o,
           na_w_in, na_rpb, na_w_o,
           ret_w_in, ret_decay_log2, ret_w_o):
    b, s, d = x.shape
    tcx = ctx.shape[1]
    depth = mod_w.shape[0]
    assert depth == 4 and s % (NA_QROWS * GRID_W) == 0 and s // GRID_W >= NA_QROWS + NA_WIN_R
    heads = d // HEAD_DIM

    cc = jnp.concatenate([c, c_ctx[None], jnp.zeros((8 - b - 1, d), F32)], axis=0)
    mods = _mod_vectors(cc, mod_w, mod_b).reshape(depth, 8, N_MOD, d)

    h = x.reshape(b * s, d)
    hc = ctx.reshape(b * tcx, d)

    for i in range(depth):
        kind = i % 4
        last = i == depth - 1
        ml = [mods[i, :b, j][:, None, :] for j in range(N_MOD)]
        mc = [mods[i, b:b + 1, j][:, None, :] for j in range(N_MOD)]
        gn = [norm_g[i, j][None, :] for j in range(6)]

        def ffn(hh, m, f):
            return _ffn(hh, m[3 * f], m[3 * f + 1], m[3 * f + 2], gn[2 * f], gn[2 * f + 1],
                        ffn_w_gate[i, f // 2].astype(BF16), ffn_w_up[i, f // 2].astype(BF16),
                        ffn_w_down[i, f // 2].astype(BF16))

        h = ffn(h, ml, 0)
        hc = ffn(hc, mc, 0)

        def in_proj(w, out_dtype):
            zl = _proj(h, gn[2], w, shift=ml[3], scale=ml[4], out_dtype=out_dtype)
            zc = _proj(hc, gn[2], w, shift=mc[3], scale=mc[4], out_dtype=out_dtype)
            return zl, zc

        if kind == 0:
            y, yc = _mla_mixer(in_proj, b, s, tcx, heads, mla_w_in[0], mla_q_gain[0], mla_kv_gain[0],
                               mla_w_uq[0], mla_w_ukv[0])
            w_o = mla_w_o[0]
        elif kind == 1:
            y, yc = _gqa_mixer(in_proj, b, s, tcx, heads, gqa_w_in[0], gqa_q_gain[0], gqa_k_gain[0])
            w_o = gqa_w_o[0]
        elif kind == 2:
            y, yc = _na_mixer(in_proj, b, s, tcx, heads, na_w_in[0], na_rpb[0])
            w_o = na_w_o[0]
        else:
            y, yc = _ret_mixer(in_proj, b, s, tcx, ret_w_in[0], ret_decay_log2[0]), None
            w_o = ret_w_o[0]

        w_o = w_o.astype(BF16)
        h = _out_proj(y.reshape(b * s, -1), w_o, gn[3], ml[5], h)
        h = ffn(h, ml, 2)
        if not last:
            hc = _out_proj(yc.reshape(b * tcx, -1), w_o, gn[3], mc[5], hc)
            hc = ffn(hc, mc, 2)
    return h.reshape(b, s, d)
```

```python
import functools

import jax
import jax.numpy as jnp
from jax import lax
from jax.experimental import pallas as pl
from jax.experimental.pallas import tpu as pltpu

F32 = jnp.float32
BF16 = jnp.bfloat16

RMS_EPS = 1e-6
ROPE_BASE = 10000.0
FFN_RES = 0.5
GRID_W = 64
N_MOD = 9
LANES = 128
ROW_CHUNK = 64
APPLY_CHUNK = 32
HEAD_DIM = 128
MLA_ROPE = 64
RET_QK = 256
RET_V = 512
RET_CHUNK = 512
NA_WIN_R = 8
NA_WIN_C = 16
NA_QROWS = 8
NEG_INF = -1e30
LOG2_E = 1.4426950408889634
VMEM_LIMIT_BYTES = 56 * 1024 * 1024


def _cparams(semantics):
    return pltpu.CompilerParams(dimension_semantics=semantics, vmem_limit_bytes=VMEM_LIMIT_BYTES)


def _tile(n, pref):
    if n <= pref:
        return n
    t = pref
    while n % t:
        t //= 2
    return t


def _rms(x):
    return x * lax.rsqrt(jnp.mean(x * x, axis=-1, keepdims=True) + RMS_EPS)


def _silu(x):
    return x * jax.nn.sigmoid(x)


def _row_chunks(n_rows):
    step = min(ROW_CHUNK, n_rows)
    return [pl.ds(r, step) for r in range(0, n_rows, step)]


def _row_rsqrt(x_ref, r_ref):
    for rows in _row_chunks(x_ref.shape[0]):
        x = x_ref[rows, :].astype(F32)
        ms = jnp.mean(x * x, axis=-1, keepdims=True)
        r_ref[rows, :] = jnp.broadcast_to(lax.rsqrt(ms + RMS_EPS), (rows.size, LANES))


def _loop_row_chunks(n_rows, body):
    step = min(APPLY_CHUNK, n_rows)

    def it(c, carry):
        body(pl.ds(pl.multiple_of(c * step, step), step))
        return carry

    lax.fori_loop(0, n_rows // step, it, 0, unroll=2)


def _norm_modulate_rows(x_ref, u_ref, r_ref, gain, shift):
    _row_rsqrt(x_ref, r_ref)

    def apply(rows):
        u = x_ref[rows, :].astype(F32) * _lane_tile(r_ref[rows, :], x_ref.shape[1] // LANES) * gain
        u_ref[rows, :] = (u if shift is None else u + shift).astype(u_ref.dtype)

    _loop_row_chunks(x_ref.shape[0], apply)


def _norm_residual_rows(y_ref, h_ref, o_ref, r_ref, gain):
    _row_rsqrt(y_ref, r_ref)

    def apply(rows):
        y = y_ref[rows, :] * _lane_tile(r_ref[rows, :], y_ref.shape[1] // LANES)
        o_ref[rows, :] = h_ref[rows, :] + y * gain

    _loop_row_chunks(y_ref.shape[0], apply)


def _lane_tile(x, n):
    return x if n == 1 else jnp.concatenate([x] * n, axis=1)


def _mod_kernel(cc_ref, w_ref, b_ref, o_ref):
    s = _silu(cc_ref[...]).astype(BF16)
    o_ref[0] = jnp.dot(s, w_ref[0].astype(BF16), preferred_element_type=F32) + b_ref[0]


def _mod_vectors(cc, mod_w, mod_b):
    depth, d, n = mod_w.shape
    rows = cc.shape[0]
    tn = _tile(n, 1024)
    return pl.pallas_call(
        _mod_kernel,
        grid=(depth, n // tn),
        in_specs=[
            pl.BlockSpec((rows, d), lambda l, j: (0, 0)),
            pl.BlockSpec((1, d, tn), lambda l, j: (l, 0, j)),
            pl.BlockSpec((1, 1, tn), lambda l, j: (l, 0, j)),
        ],
        out_specs=pl.BlockSpec((1, rows, tn), lambda l, j: (l, 0, j)),
        out_shape=jax.ShapeDtypeStruct((depth, rows, n), F32),
        compiler_params=_cparams(("arbitrary", "arbitrary")),
        name="mod_vectors",
    )(cc, mod_w, mod_b.reshape(depth, 1, n))


def _ffn_kernel(h_ref, sh_ref, sc_ref, gt_ref, gpre_ref, gpost_ref, wg_ref, wu_ref, wd_ref,
                o_ref, u_ref, acc_ref, r_ref, *, nk):
    k = pl.program_id(1)

    @pl.when(k == 0)
    def _():
        _norm_modulate_rows(h_ref, u_ref, r_ref, gpre_ref[...] * (1.0 + sc_ref[0]), sh_ref[0])
        acc_ref[...] = jnp.zeros_like(acc_ref)

    u = u_ref[...]
    g = jnp.dot(u, wg_ref[...], preferred_element_type=F32)
    up = jnp.dot(u, wu_ref[...], preferred_element_type=F32)
    a = (_silu(g) * up).astype(BF16)
    acc_ref[...] += jnp.dot(a, wd_ref[...], preferred_element_type=F32)

    @pl.when(k == nk - 1)
    def _():
        _norm_residual_rows(acc_ref, h_ref, o_ref, r_ref, FFN_RES * gt_ref[0] * gpost_ref[...])


def _ffn(h, shift, scale, gate, g_pre, g_post, wg, wu, wd, widx, *, tm=512, tf=512):
    m, d = h.shape
    f = wg.shape[-1]
    wl, wh = widx
    bm = shift.shape[0]
    rows_per_b = m // bm
    tm = _tile(rows_per_b, tm)
    tf = _tile(f, tf)
    nk = f // tf
    per_b = rows_per_b // tm
    mod_spec = pl.BlockSpec((1, 1, d), lambda i, k: (i // per_b, 0, 0))
    vec_spec = pl.BlockSpec((1, d), lambda i, k: (0, 0))
    return pl.pallas_call(
        functools.partial(_ffn_kernel, nk=nk),
        grid=(m // tm, nk),
        in_specs=[
            pl.BlockSpec((tm, d), lambda i, k: (i, 0)),
            mod_spec, mod_spec, mod_spec, vec_spec, vec_spec,
            pl.BlockSpec((None, None, d, tf), lambda i, k: (wl, wh, 0, k)),
            pl.BlockSpec((None, None, d, tf), lambda i, k: (wl, wh, 0, k)),
            pl.BlockSpec((None, None, tf, d), lambda i, k: (wl, wh, k, 0)),
        ],
        out_specs=pl.BlockSpec((tm, d), lambda i, k: (i, 0)),
        out_shape=jax.ShapeDtypeStruct((m, d), F32),
        scratch_shapes=[pltpu.VMEM((tm, d), BF16), pltpu.VMEM((tm, d), F32), pltpu.VMEM((tm, LANES), F32)],
        compiler_params=_cparams(("parallel", "arbitrary")),
        name="ffn_half_step",
    )(h, shift, scale, gate, g_pre, g_post, wg, wu, wd)


def _proj_kernel(*refs, modulated):
    if modulated:
        x_ref, g_ref, sh_ref, sc_ref, w_ref, o_ref, u_ref, r_ref = refs
    else:
        x_ref, g_ref, w_ref, o_ref, u_ref, r_ref = refs

    @pl.when(pl.program_id(1) == 0)
    def _():
        if modulated:
            _norm_modulate_rows(x_ref, u_ref, r_ref, g_ref[...] * (1.0 + sc_ref[0]), sh_ref[0])
        else:
            _norm_modulate_rows(x_ref, u_ref, r_ref, g_ref[...], None)

    o_ref[...] = jnp.dot(u_ref[...], w_ref[...], preferred_element_type=F32).astype(o_ref.dtype)


def _proj(x, gain, w, *, out_dtype, shift=None, scale=None, kblock=0, tm=1024):
    m = x.shape[0]
    k, n = w.shape
    modulated = shift is not None
    bm = shift.shape[0] if modulated else 1
    rows_per_b = m // bm
    tm = _tile(rows_per_b, tm)
    per_b = rows_per_b // tm
    tn = n if k * n * 2 <= 6 * 1024 * 1024 else _tile(n, 1024)
    in_specs = [pl.BlockSpec((tm, k), lambda i, j: (i, kblock)),
                pl.BlockSpec((1, k), lambda i, j: (0, 0))]
    args = [x, gain]
    if modulated:
        mod_spec = pl.BlockSpec((1, 1, k), lambda i, j: (i // per_b, 0, 0))
        in_specs += [mod_spec, mod_spec]
        args += [shift, scale]
    in_specs.append(pl.BlockSpec((k, tn), lambda i, j: (0, j)))
    args.append(w)
    return pl.pallas_call(
        functools.partial(_proj_kernel, modulated=modulated),
        grid=(m // tm, n // tn),
        in_specs=in_specs,
        out_specs=pl.BlockSpec((tm, tn), lambda i, j: (i, j)),
        out_shape=jax.ShapeDtypeStruct((m, n), out_dtype),
        scratch_shapes=[pltpu.VMEM((tm, k), BF16), pltpu.VMEM((tm, LANES), F32)],
        compiler_params=_cparams(("parallel", "arbitrary")),
        name="norm_proj",
    )(*args)


def _out_proj_kernel(a_ref, w_ref, gpost_ref, gt_ref, h_ref, o_ref, acc_ref, r_ref, *, nk):
    if nk == 1:
        y = jnp.dot(a_ref[...], w_ref[...], preferred_element_type=F32)
        o_ref[...] = h_ref[...] + _rms(y) * (gt_ref[0] * gpost_ref[...])
        return
    k = pl.program_id(1)

    @pl.when(k == 0)
    def _():
        acc_ref[...] = jnp.zeros_like(acc_ref)

    acc_ref[...] += jnp.dot(a_ref[...], w_ref[...], preferred_element_type=F32)

    @pl.when(k == nk - 1)
    def _():
        _norm_residual_rows(acc_ref, h_ref, o_ref, r_ref, gt_ref[0] * gpost_ref[...])


def _out_proj(a, w, g_post, gate, h, *, tm=512, tk=2048):
    m, kdim = a.shape
    d = w.shape[1]
    bm = gate.shape[0]
    rows_per_b = m // bm
    tm = _tile(rows_per_b, tm)
    per_b = rows_per_b // tm
    tk = _tile(kdim, tk)
    nk = kdim // tk
    return pl.pallas_call(
        functools.partial(_out_proj_kernel, nk=nk),
        grid=(m // tm, nk),
        in_specs=[
            pl.BlockSpec((tm, tk), lambda i, k: (i, k)),
            pl.BlockSpec((tk, d), lambda i, k: (k, 0)),
            pl.BlockSpec((1, d), lambda i, k: (0, 0)),
            pl.BlockSpec((1, 1, d), lambda i, k: (i // per_b, 0, 0)),
            pl.BlockSpec((tm, d), lambda i, k: (i, 0)),
        ],
        out_specs=pl.BlockSpec((tm, d), lambda i, k: (i, 0)),
        out_shape=jax.ShapeDtypeStruct((m, d), F32),
        scratch_shapes=[pltpu.VMEM((tm, d), F32), pltpu.VMEM((tm, LANES), F32)],
        compiler_params=_cparams(("parallel", "arbitrary")),
        name="out_proj",
    )(a, w, g_post, gate, h)


def _rope128(x, c_ref, s_ref):
    return x * c_ref[...] + pltpu.roll(x, 64, axis=1) * s_ref[...]


def _prep_kernel(*refs, mode, rope, norm, scale, heads, xcol, x2col):
    refs = list(refs)
    o_ref = refs.pop()
    x_ref = refs.pop(0)
    x2_ref = refs.pop(0) if mode == "mla_k" else None
    g_ref = refs.pop(0) if norm else None
    c_ref, s_ref = (refs.pop(0), refs.pop(0)) if rope else (None, None)
    wx = HEAD_DIM if mode in ("head", "mla_k") else 2 * HEAD_DIM
    wo = HEAD_DIM if mode == "head" else 2 * HEAD_DIM

    if mode == "mla_k":
        shared = x2_ref[0, :, pl.ds(x2col * LANES, LANES)].astype(F32)
        if rope:
            shared = _rope128(shared, c_ref, s_ref)
        shared = shared.astype(o_ref.dtype)
    for h in range(heads):
        x = x_ref[0, :, pl.ds(xcol(h) * wx, wx)].astype(F32)
        if mode == "head":
            if norm:
                x = _rms(x) * g_ref[...]
            if rope:
                x = _rope128(x, c_ref, s_ref)
            o_ref[0, :, pl.ds(h * wo, wo)] = (x * scale).astype(o_ref.dtype)
        elif mode == "mla_q":
            r = x[:, HEAD_DIM:]
            if rope:
                r = _rope128(r, c_ref, s_ref)
            o_ref[0, :, pl.ds(h * wo, HEAD_DIM)] = (x[:, :HEAD_DIM] * scale).astype(o_ref.dtype)
            o_ref[0, :, pl.ds(h * wo + HEAD_DIM, HEAD_DIM)] = (r * scale).astype(o_ref.dtype)
        else:
            o_ref[0, :, pl.ds(h * wo, HEAD_DIM)] = x.astype(o_ref.dtype)
            o_ref[0, :, pl.ds(h * wo + HEAD_DIM, HEAD_DIM)] = shared


def _prep(x, *, mode, heads, xcol, x2=None, x2col=None, gain=None, tables=None, scale=1.0, ts=256):
    b, t, _ = x.shape
    ts = _tile(t, ts)
    wo = HEAD_DIM if mode == "head" else 2 * HEAD_DIM
    in_specs = [pl.BlockSpec((1, ts, x.shape[2]), lambda bb, i: (bb, i, 0))]
    args = [x]
    if mode == "mla_k":
        in_specs.append(pl.BlockSpec((1, ts, x2.shape[2]), lambda bb, i: (bb, i, 0)))
        args.append(x2)
    if gain is not None:
        in_specs.append(pl.BlockSpec((1, HEAD_DIM), lambda bb, i: (0, 0)))
        args.append(gain)
    if tables is not None:
        tspec = pl.BlockSpec((ts, LANES), lambda bb, i: (i, 0))
        in_specs += [tspec, tspec]
        args += list(tables)
    return pl.pallas_call(
        functools.partial(_prep_kernel, mode=mode, rope=tables is not None, norm=gain is not None,
                          scale=scale, heads=heads, xcol=xcol, x2col=x2col),
        grid=(b, t // ts),
        in_specs=in_specs,
        out_specs=pl.BlockSpec((1, ts, heads * wo), lambda bb, i: (bb, i, 0)),
        out_shape=jax.ShapeDtypeStruct((b, t, heads * wo), BF16),
        compiler_params=_cparams(("parallel", "parallel")),
        name="head_prep_" + mode,
    )(*args)


def _flash_kernel(*refs, nk, scale, has_lat, n_sub):
    if has_lat:
        q_ref, kc_ref, vc_ref, kl_ref, vl_ref, o_ref, m_ref, acc_ref, va_ref = refs
    else:
        q_ref, kc_ref, vc_ref, o_ref, m_ref, acc_ref, va_ref = refs
    kk = pl.program_id(3)
    tq = q_ref.shape[1]
    dv = o_ref.shape[2]
    tr = tq // n_sub

    def step(k, v):
        tk = k.shape[0]
        va_ref[pl.ds(0, tk), pl.ds(0, dv)] = v
        va = va_ref[pl.ds(0, tk), :]

        def scores(r):
            q = q_ref[0, pl.ds(r * tr, tr), :]
            if scale != 1.0:
                q = (q.astype(F32) * scale).astype(BF16)
            return lax.dot_general(q, k, (((1,), (1,)), ((), ())), preferred_element_type=F32)

        s_next = scores(0)
        for r in range(n_sub):
            rows = pl.ds(r * tr, tr)
            s = s_next
            if r + 1 < n_sub:
                s_next = scores(r + 1)
            m_prev = m_ref[rows, :]
            m_new = jnp.maximum(m_prev, jnp.max(s, axis=-1, keepdims=True))
            alpha = jnp.exp2(m_prev - m_new)
            p = jnp.exp2((s - _lane_tile(m_new, tk // LANES)).astype(BF16))
            acc_ref[rows, :] = (_lane_tile(alpha, 2 * dv // LANES) * acc_ref[rows, :]
                                + jnp.dot(p, va, preferred_element_type=F32))
            m_ref[rows, :] = m_new

    @pl.when(kk == 0)
    def _():
        m_ref[...] = jnp.full_like(m_ref, NEG_INF)
        acc_ref[...] = jnp.zeros_like(acc_ref)
        va_ref[:, pl.ds(dv, dv)] = jnp.ones((va_ref.shape[0], dv), va_ref.dtype)
        step(kc_ref[0], vc_ref[0])

    if has_lat:
        @pl.when(kk > 0)
        def _():
            step(kl_ref[0], vl_ref[0])

    @pl.when(kk == nk - 1)
    def _():
        o_ref[0] = (acc_ref[:, pl.ds(0, dv)] / acc_ref[:, pl.ds(dv, dv)]).astype(o_ref.dtype)


def _flash(q, kc, vc, kl=None, vl=None, *, heads, dq, dv, qcol, kcol, vcol, scale=1.0,
           tq=4096, tk=2048, tr=512):
    b, s, _ = q.shape
    tc = kc.shape[1]
    tq = _tile(s, tq)
    has_lat = kl is not None
    if has_lat:
        tk = _tile(kl.shape[1], tk)
        nk = 1 + kl.shape[1] // tk
    else:
        nk = 1
    in_specs = [
        pl.BlockSpec((1, tq, dq), lambda bb, h, i, kk: (bb, i, qcol(h))),
        pl.BlockSpec((1, tc, dq), lambda bb, h, i, kk: (bb, 0, kcol(h))),
        pl.BlockSpec((1, tc, dv), lambda bb, h, i, kk: (bb, 0, vcol(h))),
    ]
    args = [q, kc, vc]
    if has_lat:
        in_specs += [
            pl.BlockSpec((1, tk, dq), lambda bb, h, i, kk: (bb, jnp.maximum(kk - 1, 0), kcol(h))),
            pl.BlockSpec((1, tk, dv), lambda bb, h, i, kk: (bb, jnp.maximum(kk - 1, 0), vcol(h))),
        ]
        args += [kl, vl]
    return pl.pallas_call(
        functools.partial(_flash_kernel, nk=nk, scale=scale, has_lat=has_lat, n_sub=tq // _tile(tq, tr)),
        grid=(b, heads, s // tq, nk),
        in_specs=in_specs,
        out_specs=pl.BlockSpec((1, tq, dv), lambda bb, h, i, kk: (bb, i, h)),
        out_shape=jax.ShapeDtypeStruct((b, s, heads * dv), BF16),
        scratch_shapes=[pltpu.VMEM((tq, LANES), F32), pltpu.VMEM((tq, 2 * dv), F32),
                        pltpu.VMEM((max(tc, tk), 2 * dv), BF16)],
        compiler_params=_cparams(("parallel", "parallel", "parallel", "arbitrary")),
        name="flash_attention",
    )(*args)


def _na_kernel(q_ref, k_ref, v_ref, kc_ref, vc_ref, bias_ref, o_ref, *, band, seq, scale, n_sub):
    i = pl.program_id(2)
    tq = q_ref.shape[1]
    start = jnp.clip(i * tq - (band - tq) // 2, 0, seq - band)
    start = pl.multiple_of(start, 256)
    dv = o_ref.shape[2]
    kb = k_ref[0, pl.ds(start, band), :]
    kc = kc_ref[0]
    vb = jnp.concatenate([v_ref[0, pl.ds(start, band), :], jnp.ones((band, dv), BF16)], axis=1)
    vc = jnp.concatenate([vc_ref[0], jnp.ones((kc.shape[0], dv), BF16)], axis=1)
    dn = (((1,), (1,)), ((), ()))
    tr = tq // n_sub
    for r in range(n_sub):
        rows = pl.ds(r * tr, tr)
        q = (q_ref[0, rows, :].astype(F32) * scale).astype(BF16)
        s_nb = lax.dot_general(q, kb, dn, preferred_element_type=F32) + bias_ref[0, 0, rows, :]
        s_cx = lax.dot_general(q, kc, dn, preferred_element_type=F32)
        m = jnp.maximum(jnp.max(s_nb, axis=-1, keepdims=True), jnp.max(s_cx, axis=-1, keepdims=True))
        p_nb = jnp.exp2((s_nb - m).astype(BF16))
        p_cx = jnp.exp2((s_cx - m).astype(BF16))
        o = (jnp.dot(p_nb, vb, preferred_element_type=F32) + jnp.dot(p_cx, vc, preferred_element_type=F32))
        o_ref[0, rows, :] = (o[:, :dv] / o[:, dv:]).astype(o_ref.dtype)


def _na_bias_tiles(rpb, rows):
    nr, wr, wc, w = NA_QROWS, NA_WIN_R, NA_WIN_C, GRID_W
    brows = nr + wr
    heads = rpb.shape[0]
    lpad = w - wc
    vp = jnp.pad(rpb.astype(F32) * LOG2_E, ((0, 0), (0, 0), (lpad, lpad)), constant_values=NEG_INF)
    toep = jnp.stack([vp[:, :, w - 1 - c: 2 * w - 1 - c] for c in range(w)], axis=2)
    col = jnp.arange(w, dtype=jnp.int32)
    c0 = jnp.clip(col - wc // 2, 0, w - wc)
    in_col = (col[None, :] >= c0[:, None]) & (col[None, :] < c0[:, None] + wc)
    toep = jnp.where(in_col[None, None], toep, NEG_INF)
    masked = jnp.full((heads, 1, w, w), NEG_INF, F32)
    tiles = []
    for r_first, b_first in ((0, 0), (nr, nr - wr // 2), (rows - nr, rows - brows)):
        per_row = []
        for a in range(nr):
            r = r_first + a
            r0 = min(max(r - wr // 2, 0), rows - wr)
            lo = r0 - b_first
            dr_lo = r0 - r + (wr - 1)
            blk = toep[:, dr_lo: dr_lo + wr]
            blk = jnp.concatenate([jnp.tile(masked, (1, lo, 1, 1)), blk,
                                   jnp.tile(masked, (1, brows - wr - lo, 1, 1))], axis=1)
            per_row.append(jnp.swapaxes(blk, 1, 2))
        tiles.append(jnp.stack(per_row, axis=1).reshape(heads, nr * w, brows * w))
    return jnp.stack(tiles, axis=1)


def _na_attention(z, zc, rpb, *, heads):
    b, s, _ = z.shape
    tcx = zc.shape[1]
    rows = s // GRID_W
    tq = NA_QROWS * GRID_W
    band = (NA_QROWS + NA_WIN_R) * GRID_W
    nblk = s // tq
    bias = _na_bias_tiles(rpb, rows)
    scale = HEAD_DIM ** -0.5 * LOG2_E

    def variant(i):
        return jnp.where(i == 0, 0, jnp.where(i == nblk - 1, 2, 1))

    return pl.pallas_call(
        functools.partial(_na_kernel, band=band, seq=s, scale=scale, n_sub=4),
        grid=(b, heads, nblk),
        in_specs=[
            pl.BlockSpec((1, tq, HEAD_DIM), lambda bb, h, i: (bb, i, h)),
            pl.BlockSpec((1, s, HEAD_DIM), lambda bb, h, i: (bb, 0, heads + h)),
            pl.BlockSpec((1, s, HEAD_DIM), lambda bb, h, i: (bb, 0, 2 * heads + h)),
            pl.BlockSpec((1, tcx, HEAD_DIM), lambda bb, h, i: (bb, 0, heads + h)),
            pl.BlockSpec((1, tcx, HEAD_DIM), lambda bb, h, i: (bb, 0, 2 * heads + h)),
            pl.BlockSpec((1, 1, tq, band), lambda bb, h, i: (h, variant(i), 0, 0)),
        ],
        out_specs=pl.BlockSpec((1, tq, HEAD_DIM), lambda bb, h, i: (bb, i, h)),
        out_shape=jax.ShapeDtypeStruct((b, s, heads * HEAD_DIM), BF16),
        compiler_params=_cparams(("parallel", "parallel", "arbitrary")),
        name="neighbourhood_attention",
    )(z, z, z, zc, zc, bias)


def _ret_kernel(dec_ref, ql_ref, kl_ref, vl_ref, kc_ref, vc_ref, cos_ref, sin_ref, o_ref, st_ref, din_ref,
                *, nc):
    h = pl.program_id(1)
    d = pl.program_id(2)
    t = pl.program_id(3)
    half = RET_QK // 2
    lg = jnp.log1p(-jnp.exp2(jnp.full((1, 1), dec_ref[d, h], F32)))
    fwd = d == 0

    def rope(x):
        c, s = cos_ref[...], sin_ref[...]
        x1, x2 = x[:, :half], x[:, half:]
        return jnp.concatenate([x1 * c - x2 * s, x2 * c + x1 * s], axis=-1)

    def update(k, v):
        n = k.shape[0]
        pos = lax.broadcasted_iota(jnp.int32, (n, 1), 0).astype(F32)
        zeta = jnp.exp(jnp.where(fwd, n - 1.0 - pos, pos) * lg)
        kz = (k * zeta).astype(BF16)
        kv = lax.dot_general(kz, v, (((0,), (0,)), ((), ())), preferred_element_type=F32)
        st_ref[...] = st_ref[...] * jnp.exp(n * lg) + kv

    @pl.when(t == 0)
    def _():
        st_ref[...] = jnp.zeros_like(st_ref)
        n = din_ref.shape[0]
        ri = lax.broadcasted_iota(jnp.int32, (n, n), 0)
        ci = lax.broadcasted_iota(jnp.int32, (n, n), 1)
        dist = jnp.where(fwd, ri - ci, ci - ri)
        din_ref[...] = jnp.where(dist >= 0, jnp.exp(jnp.maximum(dist, 0).astype(F32) * lg), 0.0)

    @pl.when(t < nc)
    def _():
        update(kc_ref[0].astype(F32) * RET_QK ** -0.5, vc_ref[0])

    @pl.when(t >= nc)
    def _():
        q = rope(ql_ref[0].astype(F32))
        k = rope(kl_ref[0].astype(F32) * RET_QK ** -0.5)
        v = vl_ref[0]
        n = q.shape[0]
        pos = lax.broadcasted_iota(jnp.int32, (n, 1), 0).astype(F32)
        qb = q.astype(BF16)
        s = lax.dot_general(qb, k.astype(BF16), (((1,), (1,)), ((), ())),
                            preferred_element_type=F32) * din_ref[...]
        xi = jnp.exp(jnp.where(fwd, pos + 1.0, n - pos) * lg)
        o = (jnp.dot(s.astype(BF16), v, preferred_element_type=F32)
             + jnp.dot(qb, st_ref[...].astype(BF16), preferred_element_type=F32) * xi)
        o_ref[0, 0] = o.astype(o_ref.dtype)
        update(k, v)


def _retention(z, zc, decay_log2, cos, sin, *, heads):
    b, s, _ = z.shape
    tcx = zc.shape[1]
    chunk = _tile(s, RET_CHUNK)
    cchunk = _tile(tcx, RET_CHUNK)
    nc, nl = tcx // cchunk, s // chunk
    kcol0 = heads
    vcol0 = (2 * heads * RET_QK) // RET_V

    def cidx(d, t):
        c = jnp.minimum(t, nc - 1)
        return jnp.where(d == 0, c, nc - 1 - c)

    def lidx(d, t):
        c = jnp.maximum(t - nc, 0)
        return jnp.where(d == 0, c, nl - 1 - c)

    return pl.pallas_call(
        functools.partial(_ret_kernel, nc=nc),
        grid=(b, heads, 2, nc + nl),
        in_specs=[
            pl.BlockSpec(memory_space=pltpu.SMEM),
            pl.BlockSpec((1, chunk, RET_QK), lambda bb, h, d, t: (bb, lidx(d, t), h)),
            pl.BlockSpec((1, chunk, RET_QK), lambda bb, h, d, t: (bb, lidx(d, t), kcol0 + h)),
            pl.BlockSpec((1, chunk, RET_V), lambda bb, h, d, t: (bb, lidx(d, t), vcol0 + h)),
            pl.BlockSpec((1, cchunk, RET_QK), lambda bb, h, d, t: (bb, cidx(d, t), kcol0 + h)),
            pl.BlockSpec((1, cchunk, RET_V), lambda bb, h, d, t: (bb, cidx(d, t), vcol0 + h)),
            pl.BlockSpec((chunk, RET_QK // 2), lambda bb, h, d, t: (lidx(d, t), 0)),
            pl.BlockSpec((chunk, RET_QK // 2), lambda bb, h, d, t: (lidx(d, t), 0)),
        ],
        out_specs=pl.BlockSpec((1, 1, chunk, RET_V), lambda bb, h, d, t: (d, bb, lidx(d, t), h)),
        out_shape=jax.ShapeDtypeStruct((2, b, s, heads * RET_V), BF16),
        scratch_shapes=[pltpu.VMEM((RET_QK, RET_V), F32), pltpu.VMEM((chunk, chunk), F32)],
        compiler_params=_cparams(("parallel", "parallel", "arbitrary", "arbitrary")),
        name="retention_scan",
    )(decay_log2, z, z, z, zc, zc, cos, sin)


def _ret_gate_kernel(of_ref, ob_ref, g_ref, y_ref, *, heads):
    for h in range(heads):
        cols = pl.ds(h * RET_V, RET_V)
        o = of_ref[0, 0, :, cols].astype(F32) + ob_ref[0, 0, :, cols].astype(F32)
        y_ref[0, :, cols] = (_silu(g_ref[0, :, cols].astype(F32)) * _rms(o)).astype(y_ref.dtype)


def _ret_gate(o2, z, *, heads, ts=256):
    _, b, s, width = o2.shape
    ts = _tile(s, ts)
    gcol = (2 * heads * RET_QK + heads * RET_V) // width
    return pl.pallas_call(
        functools.partial(_ret_gate_kernel, heads=heads),
        grid=(b, s // ts),
        in_specs=[
            pl.BlockSpec((1, 1, ts, width), lambda bb, i: (0, bb, i, 0)),
            pl.BlockSpec((1, 1, ts, width), lambda bb, i: (1, bb, i, 0)),
            pl.BlockSpec((1, ts, width), lambda bb, i: (bb, i, gcol)),
        ],
        out_specs=pl.BlockSpec((1, ts, width), lambda bb, i: (bb, i, 0)),
        out_shape=jax.ShapeDtypeStruct((b, s, width), BF16),
        compiler_params=_cparams(("parallel", "parallel")),
        name="retention_gate",
    )(o2, o2, z)


def _axial_angles(n_tokens, rot_dim):
    t = jnp.arange(n_tokens, dtype=jnp.int32)
    row = (t // GRID_W).astype(F32)
    col = (t % GRID_W).astype(F32)
    n_f = rot_dim // 4
    freqs = ROPE_BASE ** (-jnp.arange(n_f, dtype=F32) / n_f)
    return jnp.concatenate([row[:, None] * freqs, col[:, None] * freqs], axis=-1)


def _rope_slot_tables(ang):
    pad = jnp.zeros((ang.shape[0], 64 - ang.shape[1]), F32)
    cos, sin = jnp.cos(ang), jnp.sin(ang)
    return (jnp.concatenate([cos, pad, cos, pad], axis=-1),
            jnp.concatenate([-sin, pad, sin, pad], axis=-1))


def _spread_rope_cols(w):
    half = MLA_ROPE // 2
    z = jnp.zeros(w.shape[:-1] + (64 - half,), w.dtype)
    return jnp.concatenate([w[..., :half], z, w[..., half:], z], axis=-1)


def _mla_mixer(in_proj, b, s, tcx, heads, w_in, q_gain, kv_gain, w_uq, w_ukv):
    rq = q_gain.shape[-1]
    w_in = jnp.concatenate([w_in[:, :2 * rq], _spread_rope_cols(w_in[:, 2 * rq:])], axis=-1).astype(BF16)
    w_uq = w_uq.reshape(rq, heads, HEAD_DIM + MLA_ROPE)
    w_uq = jnp.concatenate([w_uq[..., :HEAD_DIM], _spread_rope_cols(w_uq[..., HEAD_DIM:])],
                           axis=-1).reshape(rq, heads * 2 * HEAD_DIM).astype(BF16)
    w_ukv = w_ukv.astype(BF16)
    tables = _rope_slot_tables(_axial_angles(s, MLA_ROPE))
    scale = (HEAD_DIM + MLA_ROPE) ** -0.5 * LOG2_E
    zl, zc = in_proj(w_in, F32)
    ropecol = 2 * rq // LANES

    def qkv(z, n_tok, tabs):
        q = _proj(z, q_gain[None], w_uq, kblock=0, out_dtype=BF16).reshape(b, n_tok, -1)
        kv = _proj(z, kv_gain[None], w_ukv, kblock=1, out_dtype=BF16).reshape(b, n_tok, -1)
        q = _prep(q, mode="mla_q", heads=heads, xcol=lambda hh: hh, tables=tabs, scale=scale)
        k = _prep(kv, mode="mla_k", heads=heads, xcol=lambda hh: 2 * hh,
                  x2=z.reshape(b, n_tok, -1), x2col=ropecol, tables=tabs)
        return q, k, kv

    ql, kl, kvl = qkv(zl, s, tables)
    qc, kc, kvc = qkv(zc, tcx, None)
    cols = dict(heads=heads, dq=2 * HEAD_DIM, dv=HEAD_DIM, qcol=lambda hh: hh, kcol=lambda hh: hh,
                vcol=lambda hh: 2 * hh + 1)
    return _flash(ql, kc, kvc, kl, kvl, **cols), _flash(qc, kc, kvc, **cols)


def _gqa_mixer(in_proj, b, s, tcx, heads, w_in, q_gain, k_gain):
    hk = (w_in.shape[-1] // HEAD_DIM - heads) // 2
    grp = heads // hk
    tables = _rope_slot_tables(_axial_angles(s, HEAD_DIM))
    zl, zc = in_proj(w_in.astype(BF16), BF16)
    zl = zl.reshape(b, s, -1)
    zc = zc.reshape(b, tcx, -1)
    scale = HEAD_DIM ** -0.5 * LOG2_E
    qg, kg = q_gain[None], k_gain[None]
    ql = _prep(zl, mode="head", heads=heads, xcol=lambda hh: hh, gain=qg, tables=tables, scale=scale)
    kl = _prep(zl, mode="head", heads=hk, xcol=lambda hh: heads + hh, gain=kg, tables=tables)
    qc = _prep(zc, mode="head", heads=heads, xcol=lambda hh: hh, gain=qg, scale=scale)
    kc = _prep(zc, mode="head", heads=hk, xcol=lambda hh: heads + hh, gain=kg)
    cols = dict(heads=heads, dq=HEAD_DIM, dv=HEAD_DIM, qcol=lambda hh: hh,
                kcol=lambda hh: hh // grp, vcol=lambda hh: heads + hk + hh // grp)
    return _flash(ql, kc, zc, kl, zl, **cols), _flash(qc, kc, zc, **cols)


def _na_mixer(in_proj, b, s, tcx, heads, w_in, rpb):
    zl, zc = in_proj(w_in.astype(BF16), BF16)
    zl = zl.reshape(b, s, -1)
    zc = zc.reshape(b, tcx, -1)
    y = _na_attention(zl, zc, rpb, heads=heads)
    yc = _flash(zc, zc, zc, heads=heads, dq=HEAD_DIM, dv=HEAD_DIM, qcol=lambda hh: hh,
                kcol=lambda hh: heads + hh, vcol=lambda hh: 2 * heads + hh,
                scale=HEAD_DIM ** -0.5 * LOG2_E)
    return y, yc


def _ret_mixer(in_proj, b, s, tcx, w_in, decay_log2):
    rh = decay_log2.shape[-1]
    pos = jnp.arange(s, dtype=F32)
    freqs = ROPE_BASE ** (-jnp.arange(RET_QK // 2, dtype=F32) / (RET_QK // 2))
    ang = pos[:, None] * freqs
    zl, zc = in_proj(w_in.astype(BF16), BF16)
    zl = zl.reshape(b, s, -1)
    zc = zc.reshape(b, tcx, -1)
    o2 = _retention(zl, zc, decay_log2.astype(F32), jnp.cos(ang), jnp.sin(ang), heads=rh)
    return _ret_gate(o2, zl, heads=rh)


def kernel(x, c, ctx, c_ctx, mod_w, mod_b, norm_g, ffn_w_gate, ffn_w_up, ffn_w_down,
           mla_w_in, mla_q_gain, mla_kv_gain, mla_w_uq, mla_w_ukv, mla_w_o,
           gqa_w_in, gqa_q_gain, gqa_k_gain, gqa_w_o,
           na_w_in, na_rpb, na_w_o,
           ret_w_in, ret_decay_log2, ret_w_o):
    b, s, d = x.shape
    tcx = ctx.shape[1]
    depth = mod_w.shape[0]
    assert depth == 4 and s % (NA_QROWS * GRID_W) == 0 and s // GRID_W >= NA_QROWS + NA_WIN_R
    heads = d // HEAD_DIM

    cc = jnp.concatenate([c, c_ctx[None], jnp.zeros((8 - b - 1, d), F32)], axis=0)
    mods = _mod_vectors(cc, mod_w, mod_b).reshape(depth, 8, N_MOD, d)

    h = x.reshape(b * s, d)
    hc = ctx.reshape(b * tcx, d)
    w_gate, w_up, w_down = ffn_w_gate.astype(BF16), ffn_w_up.astype(BF16), ffn_w_down.astype(BF16)

    for i in range(depth):
        kind = i % 4
        last = i == depth - 1
        ml = [mods[i, :b, j][:, None, :] for j in range(N_MOD)]
        mc = [mods[i, b:b + 1, j][:, None, :] for j in range(N_MOD)]
        gn = [norm_g[i, j][None, :] for j in range(6)]

        def ffn(hh, m, f):
            return _ffn(hh, m[3 * f], m[3 * f + 1], m[3 * f + 2], gn[2 * f], gn[2 * f + 1],
                        w_gate, w_up, w_down, (i, f // 2))

        h = ffn(h, ml, 0)
        hc = ffn(hc, mc, 0)

        def in_proj(w, out_dtype):
            zl = _proj(h, gn[2], w, shift=ml[3], scale=ml[4], out_dtype=out_dtype)
            zc = _proj(hc, gn[2], w, shift=mc[3], scale=mc[4], out_dtype=out_dtype)
            return zl, zc

        if kind == 0:
            y, yc = _mla_mixer(in_proj, b, s, tcx, heads, mla_w_in[0], mla_q_gain[0], mla_kv_gain[0],
                               mla_w_uq[0], mla_w_ukv[0])
            w_o = mla_w_o[0]
        elif kind == 1:
            y, yc = _gqa_mixer(in_proj, b, s, tcx, heads, gqa_w_in[0], gqa_q_gain[0], gqa_k_gain[0])
            w_o = gqa_w_o[0]
        elif kind == 2:
            y, yc = _na_mixer(in_proj, b, s, tcx, heads, na_w_in[0], na_rpb[0])
            w_o = na_w_o[0]
        else:
            y, yc = _ret_mixer(in_proj, b, s, tcx, ret_w_in[0], ret_decay_log2[0]), None
            w_o = ret_w_o[0]

        w_o = w_o.astype(BF16)
        h = _out_proj(y.reshape(b * s, -1), w_o, gn[3], ml[5], h)
        h = ffn(h, ml, 2)
        if not last:
            hc = _out_proj(yc.reshape(b * tcx, -1), w_o, gn[3], mc[5], hc)
            hc = ffn(hc, mc, 2)
    return h.reshape(b, s, d)
```

```python
import functools

import jax
import jax.numpy as jnp
from jax import lax
from jax.experimental import pallas as pl
from jax.experimental.pallas import tpu as pltpu

F32 = jnp.float32
BF16 = jnp.bfloat16

RMS_EPS = 1e-6
ROPE_BASE = 10000.0
FFN_RES = 0.5
GRID_W = 64
N_MOD = 9
LANES = 128
ROW_CHUNK = 64
APPLY_CHUNK = 32
HEAD_DIM = 128
MLA_ROPE = 64
RET_QK = 256
RET_V = 512
RET_CHUNK = 512
NA_WIN_R = 8
NA_WIN_C = 16
NA_QROWS = 8
NEG_INF = -1e30
LOG2_E = 1.4426950408889634
VMEM_LIMIT_BYTES = 56 * 1024 * 1024


def _cparams(semantics):
    return pltpu.CompilerParams(dimension_semantics=semantics, vmem_limit_bytes=VMEM_LIMIT_BYTES)


def _tile(n, pref):
    if n <= pref:
        return n
    t = pref
    while n % t:
        t //= 2
    return t


def _rms(x):
    return x * lax.rsqrt(jnp.mean(x * x, axis=-1, keepdims=True) + RMS_EPS)


def _silu(x):
    return x * jax.nn.sigmoid(x)


def _row_chunks(n_rows):
    step = min(ROW_CHUNK, n_rows)
    return [pl.ds(r, step) for r in range(0, n_rows, step)]


def _row_rsqrt(x_ref, r_ref):
    for rows in _row_chunks(x_ref.shape[0]):
        x = x_ref[rows, :].astype(F32)
        ms = jnp.mean(x * x, axis=-1, keepdims=True)
        r_ref[rows, :] = jnp.broadcast_to(lax.rsqrt(ms + RMS_EPS), (rows.size, LANES))


def _loop_row_chunks(n_rows, body):
    step = min(APPLY_CHUNK, n_rows)

    def it(c, carry):
        body(pl.ds(pl.multiple_of(c * step, step), step))
        return carry

    lax.fori_loop(0, n_rows // step, it, 0, unroll=2)


def _norm_modulate_rows(x_ref, u_ref, r_ref, gain, shift):
    _row_rsqrt(x_ref, r_ref)

    def apply(rows):
        u = x_ref[rows, :].astype(F32) * _lane_tile(r_ref[rows, :], x_ref.shape[1] // LANES) * gain
        u_ref[rows, :] = (u if shift is None else u + shift).astype(u_ref.dtype)

    _loop_row_chunks(x_ref.shape[0], apply)


def _norm_residual_rows(y_ref, h_ref, o_ref, r_ref, gain):
    _row_rsqrt(y_ref, r_ref)

    def apply(rows):
        y = y_ref[rows, :] * _lane_tile(r_ref[rows, :], y_ref.shape[1] // LANES)
        o_ref[rows, :] = h_ref[rows, :] + y * gain

    _loop_row_chunks(y_ref.shape[0], apply)


def _lane_tile(x, n):
    return x if n == 1 else jnp.concatenate([x] * n, axis=1)


def _mod_kernel(cc_ref, w_ref, b_ref, o_ref):
    s = _silu(cc_ref[...]).astype(BF16)
    o_ref[0] = jnp.dot(s, w_ref[0].astype(BF16), preferred_element_type=F32) + b_ref[0]


def _mod_vectors(cc, mod_w, mod_b):
    depth, d, n = mod_w.shape
    rows = cc.shape[0]
    tn = _tile(n, 1024)
    return pl.pallas_call(
        _mod_kernel,
        grid=(depth, n // tn),
        in_specs=[
            pl.BlockSpec((rows, d), lambda l, j: (0, 0)),
            pl.BlockSpec((1, d, tn), lambda l, j: (l, 0, j)),
            pl.BlockSpec((1, 1, tn), lambda l, j: (l, 0, j)),
        ],
        out_specs=pl.BlockSpec((1, rows, tn), lambda l, j: (l, 0, j)),
        out_shape=jax.ShapeDtypeStruct((depth, rows, n), F32),
        compiler_params=_cparams(("arbitrary", "arbitrary")),
        name="mod_vectors",
    )(cc, mod_w, mod_b.reshape(depth, 1, n))


def _ffn_kernel(h_ref, sh_ref, sc_ref, gt_ref, gpre_ref, gpost_ref, wg_ref, wu_ref, wd_ref,
                o_ref, u_ref, acc_ref, r_ref, *, nk):
    k = pl.program_id(1)

    @pl.when(k == 0)
    def _():
        _norm_modulate_rows(h_ref, u_ref, r_ref, gpre_ref[...] * (1.0 + sc_ref[0]), sh_ref[0])
        acc_ref[...] = jnp.zeros_like(acc_ref)

    u = u_ref[...]
    g = jnp.dot(u, wg_ref[...], preferred_element_type=F32)
    up = jnp.dot(u, wu_ref[...], preferred_element_type=F32)
    a = (_silu(g) * up).astype(BF16)
    acc_ref[...] += jnp.dot(a, wd_ref[...], preferred_element_type=F32)

    @pl.when(k == nk - 1)
    def _():
        _norm_residual_rows(acc_ref, h_ref, o_ref, r_ref, FFN_RES * gt_ref[0] * gpost_ref[...])


def _ffn(h, shift, scale, gate, g_pre, g_post, wg, wu, wd, widx, *, tm=512, tf=512):
    m, d = h.shape
    f = wg.shape[-1]
    wl, wh = widx
    bm = shift.shape[0]
    rows_per_b = m // bm
    tm = _tile(rows_per_b, tm)
    tf = _tile(f, tf)
    nk = f // tf
    per_b = rows_per_b // tm
    mod_spec = pl.BlockSpec((1, 1, d), lambda i, k: (i // per_b, 0, 0))
    vec_spec = pl.BlockSpec((1, d), lambda i, k: (0, 0))
    return pl.pallas_call(
        functools.partial(_ffn_kernel, nk=nk),
        grid=(m // tm, nk),
        in_specs=[
            pl.BlockSpec((tm, d), lambda i, k: (i, 0)),
            mod_spec, mod_spec, mod_spec, vec_spec, vec_spec,
            pl.BlockSpec((None, None, d, tf), lambda i, k: (wl, wh, 0, k)),
            pl.BlockSpec((None, None, d, tf), lambda i, k: (wl, wh, 0, k)),
            pl.BlockSpec((None, None, tf, d), lambda i, k: (wl, wh, k, 0)),
        ],
        out_specs=pl.BlockSpec((tm, d), lambda i, k: (i, 0)),
        out_shape=jax.ShapeDtypeStruct((m, d), F32),
        scratch_shapes=[pltpu.VMEM((tm, d), BF16), pltpu.VMEM((tm, d), F32), pltpu.VMEM((tm, LANES), F32)],
        compiler_params=_cparams(("parallel", "arbitrary")),
        name="ffn_half_step",
    )(h, shift, scale, gate, g_pre, g_post, wg, wu, wd)


def _proj_kernel(*refs, modulated):
    if modulated:
        x_ref, g_ref, sh_ref, sc_ref, w_ref, o_ref, u_ref, r_ref = refs
    else:
        x_ref, g_ref, w_ref, o_ref, u_ref, r_ref = refs

    @pl.when(pl.program_id(1) == 0)
    def _():
        if modulated:
            _norm_modulate_rows(x_ref, u_ref, r_ref, g_ref[...] * (1.0 + sc_ref[0]), sh_ref[0])
        else:
            _norm_modulate_rows(x_ref, u_ref, r_ref, g_ref[...], None)

    o_ref[...] = jnp.dot(u_ref[...], w_ref[...], preferred_element_type=F32).astype(o_ref.dtype)


def _proj(x, gain, w, *, out_dtype, shift=None, scale=None, kblock=0, tm=1024):
    m = x.shape[0]
    k, n = w.shape
    modulated = shift is not None
    bm = shift.shape[0] if modulated else 1
    rows_per_b = m // bm
    tm = _tile(rows_per_b, tm)
    per_b = rows_per_b // tm
    tn = n if k * n * 2 <= 6 * 1024 * 1024 else _tile(n, 1024)
    in_specs = [pl.BlockSpec((tm, k), lambda i, j: (i, kblock)),
                pl.BlockSpec((1, k), lambda i, j: (0, 0))]
    args = [x, gain]
    if modulated:
        mod_spec = pl.BlockSpec((1, 1, k), lambda i, j: (i // per_b, 0, 0))
        in_specs += [mod_spec, mod_spec]
        args += [shift, scale]
    in_specs.append(pl.BlockSpec((k, tn), lambda i, j: (0, j)))
    args.append(w)
    return pl.pallas_call(
        functools.partial(_proj_kernel, modulated=modulated),
        grid=(m // tm, n // tn),
        in_specs=in_specs,
        out_specs=pl.BlockSpec((tm, tn), lambda i, j: (i, j)),
        out_shape=jax.ShapeDtypeStruct((m, n), out_dtype),
        scratch_shapes=[pltpu.VMEM((tm, k), BF16), pltpu.VMEM((tm, LANES), F32)],
        compiler_params=_cparams(("parallel", "arbitrary")),
        name="norm_proj",
    )(*args)


def _out_proj_kernel(a_ref, w_ref, gpost_ref, gt_ref, h_ref, o_ref, acc_ref, r_ref, *, nk):
    if nk == 1:
        y = jnp.dot(a_ref[...], w_ref[...], preferred_element_type=F32)
        o_ref[...] = h_ref[...] + _rms(y) * (gt_ref[0] * gpost_ref[...])
        return
    k = pl.program_id(1)

    @pl.when(k == 0)
    def _():
        acc_ref[...] = jnp.zeros_like(acc_ref)

    acc_ref[...] += jnp.dot(a_ref[...], w_ref[...], preferred_element_type=F32)

    @pl.when(k == nk - 1)
    def _():
        _norm_residual_rows(acc_ref, h_ref, o_ref, r_ref, gt_ref[0] * gpost_ref[...])


def _out_proj(a, w, g_post, gate, h, *, tm=512, tk=2048):
    m, kdim = a.shape
    d = w.shape[1]
    bm = gate.shape[0]
    rows_per_b = m // bm
    tm = _tile(rows_per_b, tm)
    per_b = rows_per_b // tm
    tk = _tile(kdim, tk)
    nk = kdim // tk
    return pl.pallas_call(
        functools.partial(_out_proj_kernel, nk=nk),
        grid=(m // tm, nk),
        in_specs=[
            pl.BlockSpec((tm, tk), lambda i, k: (i, k)),
            pl.BlockSpec((tk, d), lambda i, k: (k, 0)),
            pl.BlockSpec((1, d), lambda i, k: (0, 0)),
            pl.BlockSpec((1, 1, d), lambda i, k: (i // per_b, 0, 0)),
            pl.BlockSpec((tm, d), lambda i, k: (i, 0)),
        ],
        out_specs=pl.BlockSpec((tm, d), lambda i, k: (i, 0)),
        out_shape=jax.ShapeDtypeStruct((m, d), F32),
        scratch_shapes=[pltpu.VMEM((tm, d), F32), pltpu.VMEM((tm, LANES), F32)],
        compiler_params=_cparams(("parallel", "arbitrary")),
        name="out_proj",
    )(a, w, g_post, gate, h)


def _rope128(x, c_ref, s_ref):
    return x * c_ref[...] + pltpu.roll(x, 64, axis=1) * s_ref[...]


def _prep_kernel(*refs, mode, rope, norm, scale, heads, xcol, x2col):
    refs = list(refs)
    o_ref = refs.pop()
    x_ref = refs.pop(0)
    x2_ref = refs.pop(0) if mode == "mla_k" else None
    g_ref = refs.pop(0) if norm else None
    c_ref, s_ref = (refs.pop(0), refs.pop(0)) if rope else (None, None)
    wx = HEAD_DIM if mode in ("head", "mla_k") else 2 * HEAD_DIM
    wo = HEAD_DIM if mode == "head" else 2 * HEAD_DIM

    if mode == "mla_k":
        shared = x2_ref[0, :, pl.ds(x2col * LANES, LANES)].astype(F32)
        if rope:
            shared = _rope128(shared, c_ref, s_ref)
        shared = shared.astype(o_ref.dtype)
    for h in range(heads):
        x = x_ref[0, :, pl.ds(xcol(h) * wx, wx)].astype(F32)
        if mode == "head":
            if norm:
                x = _rms(x) * g_ref[...]
            if rope:
                x = _rope128(x, c_ref, s_ref)
            o_ref[0, :, pl.ds(h * wo, wo)] = (x * scale).astype(o_ref.dtype)
        elif mode == "mla_q":
            r = x[:, HEAD_DIM:]
            if rope:
                r = _rope128(r, c_ref, s_ref)
            o_ref[0, :, pl.ds(h * wo, HEAD_DIM)] = (x[:, :HEAD_DIM] * scale).astype(o_ref.dtype)
            o_ref[0, :, pl.ds(h * wo + HEAD_DIM, HEAD_DIM)] = (r * scale).astype(o_ref.dtype)
        else:
            o_ref[0, :, pl.ds(h * wo, HEAD_DIM)] = x.astype(o_ref.dtype)
            o_ref[0, :, pl.ds(h * wo + HEAD_DIM, HEAD_DIM)] = shared


def _prep(x, *, mode, heads, xcol, x2=None, x2col=None, gain=None, tables=None, scale=1.0, ts=256):
    b, t, _ = x.shape
    ts = _tile(t, ts)
    wo = HEAD_DIM if mode == "head" else 2 * HEAD_DIM
    in_specs = [pl.BlockSpec((1, ts, x.shape[2]), lambda bb, i: (bb, i, 0))]
    args = [x]
    if mode == "mla_k":
        in_specs.append(pl.BlockSpec((1, ts, x2.shape[2]), lambda bb, i: (bb, i, 0)))
        args.append(x2)
    if gain is not None:
        in_specs.append(pl.BlockSpec((1, HEAD_DIM), lambda bb, i: (0, 0)))
        args.append(gain)
    if tables is not None:
        tspec = pl.BlockSpec((ts, LANES), lambda bb, i: (i, 0))
        in_specs += [tspec, tspec]
        args += list(tables)
    return pl.pallas_call(
        functools.partial(_prep_kernel, mode=mode, rope=tables is not None, norm=gain is not None,
                          scale=scale, heads=heads, xcol=xcol, x2col=x2col),
        grid=(b, t // ts),
        in_specs=in_specs,
        out_specs=pl.BlockSpec((1, ts, heads * wo), lambda bb, i: (bb, i, 0)),
        out_shape=jax.ShapeDtypeStruct((b, t, heads * wo), BF16),
        compiler_params=_cparams(("parallel", "parallel")),
        name="head_prep_" + mode,
    )(*args)


def _flash_kernel(*refs, nk, scale, has_lat, n_sub):
    if has_lat:
        q_ref, kc_ref, vc_ref, kl_ref, vl_ref, o_ref, m_ref, acc_ref, va_ref = refs
    else:
        q_ref, kc_ref, vc_ref, o_ref, m_ref, acc_ref, va_ref = refs
    kk = pl.program_id(3)
    tq = q_ref.shape[1]
    dv = o_ref.shape[2]
    tr = tq // n_sub

    def step(k, v):
        tk = k.shape[0]
        va_ref[pl.ds(0, tk), pl.ds(0, dv)] = v
        va = va_ref[pl.ds(0, tk), :]

        def scores(r):
            q = q_ref[0, pl.ds(r * tr, tr), :]
            if scale != 1.0:
                q = (q.astype(F32) * scale).astype(BF16)
            return lax.dot_general(q, k, (((1,), (1,)), ((), ())), preferred_element_type=F32)

        s_next = scores(0)
        for r in range(n_sub):
            rows = pl.ds(r * tr, tr)
            s = s_next
            if r + 1 < n_sub:
                s_next = scores(r + 1)
            m_prev = m_ref[rows, :]
            m_new = jnp.maximum(m_prev, jnp.max(s, axis=-1, keepdims=True))
            alpha = jnp.exp2(m_prev - m_new)
            p = jnp.exp2((s - _lane_tile(m_new, tk // LANES)).astype(BF16))
            acc_ref[rows, :] = (_lane_tile(alpha, 2 * dv // LANES) * acc_ref[rows, :]
                                + jnp.dot(p, va, preferred_element_type=F32))
            m_ref[rows, :] = m_new

    @pl.when(kk == 0)
    def _():
        m_ref[...] = jnp.full_like(m_ref, NEG_INF)
        acc_ref[...] = jnp.zeros_like(acc_ref)
        va_ref[:, pl.ds(dv, dv)] = jnp.ones((va_ref.shape[0], dv), va_ref.dtype)
        step(kc_ref[0], vc_ref[0])

    if has_lat:
        @pl.when(kk > 0)
        def _():
            step(kl_ref[0], vl_ref[0])

    @pl.when(kk == nk - 1)
    def _():
        o_ref[0] = (acc_ref[:, pl.ds(0, dv)] / acc_ref[:, pl.ds(dv, dv)]).astype(o_ref.dtype)


def _flash(q, kc, vc, kl=None, vl=None, *, heads, dq, dv, qcol, kcol, vcol, scale=1.0,
           tq=4096, tk=2048, tr=512):
    b, s, _ = q.shape
    tc = kc.shape[1]
    tq = _tile(s, tq)
    has_lat = kl is not None
    if has_lat:
        tk = _tile(kl.shape[1], tk)
        nk = 1 + kl.shape[1] // tk
    else:
        nk = 1
    in_specs = [
        pl.BlockSpec((1, tq, dq), lambda bb, h, i, kk: (bb, i, qcol(h))),
        pl.BlockSpec((1, tc, dq), lambda bb, h, i, kk: (bb, 0, kcol(h))),
        pl.BlockSpec((1, tc, dv), lambda bb, h, i, kk: (bb, 0, vcol(h))),
    ]
    args = [q, kc, vc]
    if has_lat:
        in_specs += [
            pl.BlockSpec((1, tk, dq), lambda bb, h, i, kk: (bb, jnp.maximum(kk - 1, 0), kcol(h))),
            pl.BlockSpec((1, tk, dv), lambda bb, h, i, kk: (bb, jnp.maximum(kk - 1, 0), vcol(h))),
        ]
        args += [kl, vl]
    return pl.pallas_call(
        functools.partial(_flash_kernel, nk=nk, scale=scale, has_lat=has_lat, n_sub=tq // _tile(tq, tr)),
        grid=(b, heads, s // tq, nk),
        in_specs=in_specs,
        out_specs=pl.BlockSpec((1, tq, dv), lambda bb, h, i, kk: (bb, i, h)),
        out_shape=jax.ShapeDtypeStruct((b, s, heads * dv), BF16),
        scratch_shapes=[pltpu.VMEM((tq, LANES), F32), pltpu.VMEM((tq, 2 * dv), F32),
                        pltpu.VMEM((max(tc, tk), 2 * dv), BF16)],
        compiler_params=_cparams(("parallel", "parallel", "parallel", "arbitrary")),
        name="flash_attention",
    )(*args)


def _na_kernel(q_ref, k_ref, v_ref, kc_ref, vc_ref, bias_ref, o_ref, *, band, seq, scale, n_sub):
    i = pl.program_id(2)
    tq = q_ref.shape[1]
    start = jnp.clip(i * tq - (band - tq) // 2, 0, seq - band)
    start = pl.multiple_of(start, 256)
    dv = o_ref.shape[2]
    kb = k_ref[0, pl.ds(start, band), :]
    kc = kc_ref[0]
    vb = jnp.concatenate([v_ref[0, pl.ds(start, band), :], jnp.ones((band, dv), BF16)], axis=1)
    vc = jnp.concatenate([vc_ref[0], jnp.ones((kc.shape[0], dv), BF16)], axis=1)
    dn = (((1,), (1,)), ((), ()))
    tr = tq // n_sub
    for r in range(n_sub):
        rows = pl.ds(r * tr, tr)
        q = (q_ref[0, rows, :].astype(F32) * scale).astype(BF16)
        s_nb = lax.dot_general(q, kb, dn, preferred_element_type=F32) + bias_ref[0, 0, rows, :]
        s_cx = lax.dot_general(q, kc, dn, preferred_element_type=F32)
        m = jnp.maximum(jnp.max(s_nb, axis=-1, keepdims=True), jnp.max(s_cx, axis=-1, keepdims=True))
        p_nb = jnp.exp2((s_nb - m).astype(BF16))
        p_cx = jnp.exp2((s_cx - m).astype(BF16))
        o = (jnp.dot(p_nb, vb, preferred_element_type=F32) + jnp.dot(p_cx, vc, preferred_element_type=F32))
        o_ref[0, rows, :] = (o[:, :dv] / o[:, dv:]).astype(o_ref.dtype)


def _na_bias_tiles(rpb, rows):
    nr, wr, wc, w = NA_QROWS, NA_WIN_R, NA_WIN_C, GRID_W
    brows = nr + wr
    heads = rpb.shape[0]
    lpad = w - wc
    vp = jnp.pad(rpb.astype(F32) * LOG2_E, ((0, 0), (0, 0), (lpad, lpad)), constant_values=NEG_INF)
    toep = jnp.stack([vp[:, :, w - 1 - c: 2 * w - 1 - c] for c in range(w)], axis=2)
    col = jnp.arange(w, dtype=jnp.int32)
    c0 = jnp.clip(col - wc // 2, 0, w - wc)
    in_col = (col[None, :] >= c0[:, None]) & (col[None, :] < c0[:, None] + wc)
    toep = jnp.where(in_col[None, None], toep, NEG_INF)
    masked = jnp.full((heads, 1, w, w), NEG_INF, F32)
    tiles = []
    for r_first, b_first in ((0, 0), (nr, nr - wr // 2), (rows - nr, rows - brows)):
        per_row = []
        for a in range(nr):
            r = r_first + a
            r0 = min(max(r - wr // 2, 0), rows - wr)
            lo = r0 - b_first
            dr_lo = r0 - r + (wr - 1)
            blk = toep[:, dr_lo: dr_lo + wr]
            blk = jnp.concatenate([jnp.tile(masked, (1, lo, 1, 1)), blk,
                                   jnp.tile(masked, (1, brows - wr - lo, 1, 1))], axis=1)
            per_row.append(jnp.swapaxes(blk, 1, 2))
        tiles.append(jnp.stack(per_row, axis=1).reshape(heads, nr * w, brows * w))
    return jnp.stack(tiles, axis=1)


def _na_attention(z, zc, rpb, *, heads):
    b, s, _ = z.shape
    tcx = zc.shape[1]
    rows = s // GRID_W
    tq = NA_QROWS * GRID_W
    band = (NA_QROWS + NA_WIN_R) * GRID_W
    nblk = s // tq
    bias = _na_bias_tiles(rpb, rows)
    scale = HEAD_DIM ** -0.5 * LOG2_E

    def variant(i):
        return jnp.where(i == 0, 0, jnp.where(i == nblk - 1, 2, 1))

    return pl.pallas_call(
        functools.partial(_na_kernel, band=band, seq=s, scale=scale, n_sub=4),
        grid=(b, heads, nblk),
        in_specs=[
            pl.BlockSpec((1, tq, HEAD_DIM), lambda bb, h, i: (bb, i, h)),
            pl.BlockSpec((1, s, HEAD_DIM), lambda bb, h, i: (bb, 0, heads + h)),
            pl.BlockSpec((1, s, HEAD_DIM), lambda bb, h, i: (bb, 0, 2 * heads + h)),
            pl.BlockSpec((1, tcx, HEAD_DIM), lambda bb, h, i: (bb, 0, heads + h)),
            pl.BlockSpec((1, tcx, HEAD_DIM), lambda bb, h, i: (bb, 0, 2 * heads + h)),
            pl.BlockSpec((1, 1, tq, band), lambda bb, h, i: (h, variant(i), 0, 0)),
        ],
        out_specs=pl.BlockSpec((1, tq, HEAD_DIM), lambda bb, h, i: (bb, i, h)),
        out_shape=jax.ShapeDtypeStruct((b, s, heads * HEAD_DIM), BF16),
        compiler_params=_cparams(("parallel", "parallel", "arbitrary")),
        name="neighbourhood_attention",
    )(z, z, z, zc, zc, bias)


def _ret_kernel(dec_ref, *refs, nc):
    ins, (of_ref, ob_ref, st_ref, din_ref) = refs[:14], refs[14:]
    dirs = (ins[:7] + (of_ref,), ins[7:] + (ob_ref,))
    h = pl.program_id(1)
    t = pl.program_id(2)
    half = RET_QK // 2
    lgs = [jnp.log1p(-jnp.exp2(jnp.full((1, 1), dec_ref[d, h], F32))) for d in (0, 1)]

    def rope(x, cos_ref, sin_ref):
        c, s = cos_ref[...], sin_ref[...]
        x1, x2 = x[:, :half], x[:, half:]
        return jnp.concatenate([x1 * c - x2 * s, x2 * c + x1 * s], axis=-1)

    def update(d, k, v):
        n = k.shape[0]
        pos = lax.broadcasted_iota(jnp.int32, (n, 1), 0).astype(F32)
        zeta = jnp.exp((n - 1.0 - pos if d == 0 else pos) * lgs[d])
        kz = (k * zeta).astype(BF16)
        kv = lax.dot_general(kz, v, (((0,), (0,)), ((), ())), preferred_element_type=F32)
        st_ref[d] = st_ref[d] * jnp.exp(n * lgs[d]) + kv

    @pl.when(t == 0)
    def _():
        st_ref[...] = jnp.zeros_like(st_ref)
        n = din_ref.shape[1]
        ri = lax.broadcasted_iota(jnp.int32, (n, n), 0)
        ci = lax.broadcasted_iota(jnp.int32, (n, n), 1)
        for d in (0, 1):
            dist = ri - ci if d == 0 else ci - ri
            din_ref[d] = jnp.where(dist >= 0, jnp.exp(jnp.maximum(dist, 0).astype(F32) * lgs[d]), 0.0)

    @pl.when(t < nc)
    def _():
        for d in (0, 1):
            kc_ref, vc_ref = dirs[d][3], dirs[d][4]
            update(d, kc_ref[0].astype(F32) * RET_QK ** -0.5, vc_ref[0])

    @pl.when(t >= nc)
    def _():
        for d in (0, 1):
            q_ref, k_ref, v_ref, _, _, cos_ref, sin_ref, o_ref = dirs[d]
            q = rope(q_ref[0].astype(F32), cos_ref, sin_ref)
            k = rope(k_ref[0].astype(F32) * RET_QK ** -0.5, cos_ref, sin_ref)
            v = v_ref[0]
            n = q.shape[0]
            pos = lax.broadcasted_iota(jnp.int32, (n, 1), 0).astype(F32)
            qb = q.astype(BF16)
            s = lax.dot_general(qb, k.astype(BF16), (((1,), (1,)), ((), ())),
                                preferred_element_type=F32) * din_ref[d]
            xi = jnp.exp((pos + 1.0 if d == 0 else n - pos) * lgs[d])
            o = (jnp.dot(s.astype(BF16), v, preferred_element_type=F32)
                 + jnp.dot(qb, st_ref[d].astype(BF16), preferred_element_type=F32) * xi)
            o_ref[0] = o.astype(o_ref.dtype)
            update(d, k, v)


def _retention(z, zc, decay_log2, cos, sin, *, heads):
    b, s, _ = z.shape
    tcx = zc.shape[1]
    chunk = _tile(s, RET_CHUNK)
    cchunk = _tile(tcx, RET_CHUNK)
    nc, nl = tcx // cchunk, s // chunk
    kcol0 = heads
    vcol0 = (2 * heads * RET_QK) // RET_V

    def cidx(d, t):
        c = jnp.minimum(t, nc - 1)
        return c if d == 0 else nc - 1 - c

    def lidx(d, t):
        c = jnp.maximum(t - nc, 0)
        return c if d == 0 else nl - 1 - c

    in_specs = [pl.BlockSpec(memory_space=pltpu.SMEM)]
    out_specs = []
    for d in (0, 1):
        in_specs += [
            pl.BlockSpec((1, chunk, RET_QK), lambda bb, h, t, d=d: (bb, lidx(d, t), h)),
            pl.BlockSpec((1, chunk, RET_QK), lambda bb, h, t, d=d: (bb, lidx(d, t), kcol0 + h)),
            pl.BlockSpec((1, chunk, RET_V), lambda bb, h, t, d=d: (bb, lidx(d, t), vcol0 + h)),
            pl.BlockSpec((1, cchunk, RET_QK), lambda bb, h, t, d=d: (bb, cidx(d, t), kcol0 + h)),
            pl.BlockSpec((1, cchunk, RET_V), lambda bb, h, t, d=d: (bb, cidx(d, t), vcol0 + h)),
            pl.BlockSpec((chunk, RET_QK // 2), lambda bb, h, t, d=d: (lidx(d, t), 0)),
            pl.BlockSpec((chunk, RET_QK // 2), lambda bb, h, t, d=d: (lidx(d, t), 0)),
        ]
        out_specs.append(pl.BlockSpec((1, chunk, RET_V), lambda bb, h, t, d=d: (bb, lidx(d, t), h)))
    out = jax.ShapeDtypeStruct((b, s, heads * RET_V), BF16)
    return pl.pallas_call(
        functools.partial(_ret_kernel, nc=nc),
        grid=(b, heads, nc + nl),
        in_specs=in_specs,
        out_specs=out_specs,
        out_shape=[out, out],
        scratch_shapes=[pltpu.VMEM((2, RET_QK, RET_V), F32), pltpu.VMEM((2, chunk, chunk), F32)],
        compiler_params=_cparams(("parallel", "parallel", "arbitrary")),
        name="retention_scan",
    )(decay_log2, *([z, z, z, zc, zc, cos, sin] * 2))


def _ret_gate_kernel(of_ref, ob_ref, g_ref, y_ref, *, heads):
    for h in range(heads):
        cols = pl.ds(h * RET_V, RET_V)
        o = of_ref[0, :, cols].astype(F32) + ob_ref[0, :, cols].astype(F32)
        y_ref[0, :, cols] = (_silu(g_ref[0, :, cols].astype(F32)) * _rms(o)).astype(y_ref.dtype)


def _ret_gate(o_fwd, o_bwd, z, *, heads, ts=256):
    b, s, width = o_fwd.shape
    ts = _tile(s, ts)
    gcol = (2 * heads * RET_QK + heads * RET_V) // width
    spec = pl.BlockSpec((1, ts, width), lambda bb, i: (bb, i, 0))
    return pl.pallas_call(
        functools.partial(_ret_gate_kernel, heads=heads),
        grid=(b, s // ts),
        in_specs=[spec, spec, pl.BlockSpec((1, ts, width), lambda bb, i: (bb, i, gcol))],
        out_specs=spec,
        out_shape=jax.ShapeDtypeStruct((b, s, width), BF16),
        compiler_params=_cparams(("parallel", "parallel")),
        name="retention_gate",
    )(o_fwd, o_bwd, z)


def _axial_angles(n_tokens, rot_dim):
    t = jnp.arange(n_tokens, dtype=jnp.int32)
    row = (t // GRID_W).astype(F32)
    col = (t % GRID_W).astype(F32)
    n_f = rot_dim // 4
    freqs = ROPE_BASE ** (-jnp.arange(n_f, dtype=F32) / n_f)
    return jnp.concatenate([row[:, None] * freqs, col[:, None] * freqs], axis=-1)


def _rope_slot_tables(ang):
    pad = jnp.zeros((ang.shape[0], 64 - ang.shape[1]), F32)
    cos, sin = jnp.cos(ang), jnp.sin(ang)
    return (jnp.concatenate([cos, pad, cos, pad], axis=-1),
            jnp.concatenate([-sin, pad, sin, pad], axis=-1))


def _spread_rope_cols(w):
    half = MLA_ROPE // 2
    z = jnp.zeros(w.shape[:-1] + (64 - half,), w.dtype)
    return jnp.concatenate([w[..., :half], z, w[..., half:], z], axis=-1)


def _mla_mixer(in_proj, b, s, tcx, heads, w_in, q_gain, kv_gain, w_uq, w_ukv):
    rq = q_gain.shape[-1]
    w_in = jnp.concatenate([w_in[:, :2 * rq], _spread_rope_cols(w_in[:, 2 * rq:])], axis=-1).astype(BF16)
    w_uq = w_uq.reshape(rq, heads, HEAD_DIM + MLA_ROPE)
    w_uq = jnp.concatenate([w_uq[..., :HEAD_DIM], _spread_rope_cols(w_uq[..., HEAD_DIM:])],
                           axis=-1).reshape(rq, heads * 2 * HEAD_DIM).astype(BF16)
    w_ukv = w_ukv.astype(BF16)
    tables = _rope_slot_tables(_axial_angles(s, MLA_ROPE))
    scale = (HEAD_DIM + MLA_ROPE) ** -0.5 * LOG2_E
    zl, zc = in_proj(w_in, F32)
    ropecol = 2 * rq // LANES

    def qkv(z, n_tok, tabs):
        q = _proj(z, q_gain[None], w_uq, kblock=0, out_dtype=BF16).reshape(b, n_tok, -1)
        kv = _proj(z, kv_gain[None], w_ukv, kblock=1, out_dtype=BF16).reshape(b, n_tok, -1)
        q = _prep(q, mode="mla_q", heads=heads, xcol=lambda hh: hh, tables=tabs, scale=scale)
        k = _prep(kv, mode="mla_k", heads=heads, xcol=lambda hh: 2 * hh,
                  x2=z.reshape(b, n_tok, -1), x2col=ropecol, tables=tabs)
        return q, k, kv

    ql, kl, kvl = qkv(zl, s, tables)
    qc, kc, kvc = qkv(zc, tcx, None)
    cols = dict(heads=heads, dq=2 * HEAD_DIM, dv=HEAD_DIM, qcol=lambda hh: hh, kcol=lambda hh: hh,
                vcol=lambda hh: 2 * hh + 1)
    return _flash(ql, kc, kvc, kl, kvl, **cols), _flash(qc, kc, kvc, **cols)


def _gqa_mixer(in_proj, b, s, tcx, heads, w_in, q_gain, k_gain):
    hk = (w_in.shape[-1] // HEAD_DIM - heads) // 2
    grp = heads // hk
    tables = _rope_slot_tables(_axial_angles(s, HEAD_DIM))
    zl, zc = in_proj(w_in.astype(BF16), BF16)
    zl = zl.reshape(b, s, -1)
    zc = zc.reshape(b, tcx, -1)
    scale = HEAD_DIM ** -0.5 * LOG2_E
    qg, kg = q_gain[None], k_gain[None]
    ql = _prep(zl, mode="head", heads=heads, xcol=lambda hh: hh, gain=qg, tables=tables, scale=scale)
    kl = _prep(zl, mode="head", heads=hk, xcol=lambda hh: heads + hh, gain=kg, tables=tables)
    qc = _prep(zc, mode="head", heads=heads, xcol=lambda hh: hh, gain=qg, scale=scale)
    kc = _prep(zc, mode="head", heads=hk, xcol=lambda hh: heads + hh, gain=kg)
    cols = dict(heads=heads, dq=HEAD_DIM, dv=HEAD_DIM, qcol=lambda hh: hh,
                kcol=lambda hh: hh // grp, vcol=lambda hh: heads + hk + hh // grp)
    return _flash(ql, kc, zc, kl, zl, **cols), _flash(qc, kc, zc, **cols)


def _na_mixer(in_proj, b, s, tcx, heads, w_in, rpb):
    zl, zc = in_proj(w_in.astype(BF16), BF16)
    zl = zl.reshape(b, s, -1)
    zc = zc.reshape(b, tcx, -1)
    y = _na_attention(zl, zc, rpb, heads=heads)
    yc = _flash(zc, zc, zc, heads=heads, dq=HEAD_DIM, dv=HEAD_DIM, qcol=lambda hh: hh,
                kcol=lambda hh: heads + hh, vcol=lambda hh: 2 * heads + hh,
                scale=HEAD_DIM ** -0.5 * LOG2_E)
    return y, yc


def _ret_mixer(in_proj, b, s, tcx, w_in, decay_log2):
    rh = decay_log2.shape[-1]
    pos = jnp.arange(s, dtype=F32)
    freqs = ROPE_BASE ** (-jnp.arange(RET_QK // 2, dtype=F32) / (RET_QK // 2))
    ang = pos[:, None] * freqs
    zl, zc = in_proj(w_in.astype(BF16), BF16)
    zl = zl.reshape(b, s, -1)
    zc = zc.reshape(b, tcx, -1)
    o_fwd, o_bwd = _retention(zl, zc, decay_log2.astype(F32), jnp.cos(ang), jnp.sin(ang), heads=rh)
    return _ret_gate(o_fwd, o_bwd, zl, heads=rh)


def kernel(x, c, ctx, c_ctx, mod_w, mod_b, norm_g, ffn_w_gate, ffn_w_up, ffn_w_down,
           mla_w_in, mla_q_gain, mla_kv_gain, mla_w_uq, mla_w_ukv, mla_w_o,
           gqa_w_in, gqa_q_gain, gqa_k_gain, gqa_w_o,
           na_w_in, na_rpb, na_w_o,
           ret_w_in, ret_decay_log2, ret_w_o):
    b, s, d = x.shape
    tcx = ctx.shape[1]
    depth = mod_w.shape[0]
    assert depth == 4 and s % (NA_QROWS * GRID_W) == 0 and s // GRID_W >= NA_QROWS + NA_WIN_R
    heads = d // HEAD_DIM

    cc = jnp.concatenate([c, c_ctx[None], jnp.zeros((8 - b - 1, d), F32)], axis=0)
    mods = _mod_vectors(cc, mod_w, mod_b).reshape(depth, 8, N_MOD, d)

    h = x.reshape(b * s, d)
    hc = ctx.reshape(b * tcx, d)
    w_gate, w_up, w_down = ffn_w_gate.astype(BF16), ffn_w_up.astype(BF16), ffn_w_down.astype(BF16)

    for i in range(depth):
        kind = i % 4
        last = i == depth - 1
        ml = [mods[i, :b, j][:, None, :] for j in range(N_MOD)]
        mc = [mods[i, b:b + 1, j][:, None, :] for j in range(N_MOD)]
        gn = [norm_g[i, j][None, :] for j in range(6)]

        def ffn(hh, m, f):
            return _ffn(hh, m[3 * f], m[3 * f + 1], m[3 * f + 2], gn[2 * f], gn[2 * f + 1],
                        w_gate, w_up, w_down, (i, f // 2))

        h = ffn(h, ml, 0)
        hc = ffn(hc, mc, 0)

        def in_proj(w, out_dtype):
            zl = _proj(h, gn[2], w, shift=ml[3], scale=ml[4], out_dtype=out_dtype)
            zc = _proj(hc, gn[2], w, shift=mc[3], scale=mc[4], out_dtype=out_dtype)
            return zl, zc

        if kind == 0:
            y, yc = _mla_mixer(in_proj, b, s, tcx, heads, mla_w_in[0], mla_q_gain[0], mla_kv_gain[0],
                               mla_w_uq[0], mla_w_ukv[0])
            w_o = mla_w_o[0]
        elif kind == 1:
            y, yc = _gqa_mixer(in_proj, b, s, tcx, heads, gqa_w_in[0], gqa_q_gain[0], gqa_k_gain[0])
            w_o = gqa_w_o[0]
        elif kind == 2:
            y, yc = _na_mixer(in_proj, b, s, tcx, heads, na_w_in[0], na_rpb[0])
            w_o = na_w_o[0]
        else:
            y, yc = _ret_mixer(in_proj, b, s, tcx, ret_w_in[0], ret_decay_log2[0]), None
            w_o = ret_w_o[0]

        w_o = w_o.astype(BF16)
        h = _out_proj(y.reshape(b * s, -1), w_o, gn[3], ml[5], h)
        h = ffn(h, ml, 2)
        if not last:
            hc = _out_proj(yc.reshape(b * tcx, -1), w_o, gn[3], mc[5], hc)
            hc = ffn(hc, mc, 2)
    return h.reshape(b, s, d)
```

```python
import functools

import jax
import jax.numpy as jnp
from jax import lax
from jax.experimental import pallas as pl
from jax.experimental.pallas import tpu as pltpu

F32 = jnp.float32
BF16 = jnp.bfloat16

RMS_EPS = 1e-6
ROPE_BASE = 10000.0
FFN_RES = 0.5
GRID_W = 64
N_MOD = 9
LANES = 128
ROW_CHUNK = 64
APPLY_CHUNK = 32
HEAD_DIM = 128
MLA_ROPE = 64
RET_QK = 256
RET_V = 512
RET_CHUNK = 512
NA_WIN_R = 8
NA_WIN_C = 16
NA_QROWS = 8
NEG_INF = -1e30
LOG2_E = 1.4426950408889634
VMEM_LIMIT_BYTES = 56 * 1024 * 1024


def _cparams(semantics):
    return pltpu.CompilerParams(dimension_semantics=semantics, vmem_limit_bytes=VMEM_LIMIT_BYTES)


def _tile(n, pref):
    if n <= pref:
        return n
    t = pref
    while n % t:
        t //= 2
    return t


def _rms(x):
    return x * lax.rsqrt(jnp.mean(x * x, axis=-1, keepdims=True) + RMS_EPS)


def _silu(x):
    return x * jax.nn.sigmoid(x)


def _row_chunks(n_rows):
    step = min(ROW_CHUNK, n_rows)
    return [pl.ds(r, step) for r in range(0, n_rows, step)]


def _row_rsqrt(x_ref, r_ref):
    for rows in _row_chunks(x_ref.shape[0]):
        x = x_ref[rows, :].astype(F32)
        ms = jnp.mean(x * x, axis=-1, keepdims=True)
        r_ref[rows, :] = jnp.broadcast_to(lax.rsqrt(ms + RMS_EPS), (rows.size, LANES))


def _loop_row_chunks(n_rows, body):
    step = min(APPLY_CHUNK, n_rows)

    def it(c, carry):
        body(pl.ds(pl.multiple_of(c * step, step), step))
        return carry

    lax.fori_loop(0, n_rows // step, it, 0, unroll=2)


def _norm_modulate_rows(x_ref, u_ref, r_ref, gain, shift):
    _row_rsqrt(x_ref, r_ref)

    def apply(rows):
        u = x_ref[rows, :].astype(F32) * _lane_tile(r_ref[rows, :], x_ref.shape[1] // LANES) * gain
        u_ref[rows, :] = (u if shift is None else u + shift).astype(u_ref.dtype)

    _loop_row_chunks(x_ref.shape[0], apply)


def _norm_residual_rows(y_ref, h_ref, o_ref, r_ref, gain):
    _row_rsqrt(y_ref, r_ref)

    def apply(rows):
        y = y_ref[rows, :] * _lane_tile(r_ref[rows, :], y_ref.shape[1] // LANES)
        o_ref[rows, :] = h_ref[rows, :] + y * gain

    _loop_row_chunks(y_ref.shape[0], apply)


def _lane_tile(x, n):
    return x if n == 1 else jnp.concatenate([x] * n, axis=1)


def _mod_kernel(cc_ref, w_ref, b_ref, o_ref):
    s = _silu(cc_ref[...]).astype(BF16)
    o_ref[0] = jnp.dot(s, w_ref[0].astype(BF16), preferred_element_type=F32) + b_ref[0]


def _mod_vectors(cc, mod_w, mod_b):
    depth, d, n = mod_w.shape
    rows = cc.shape[0]
    tn = _tile(n, 1024)
    return pl.pallas_call(
        _mod_kernel,
        grid=(depth, n // tn),
        in_specs=[
            pl.BlockSpec((rows, d), lambda l, j: (0, 0)),
            pl.BlockSpec((1, d, tn), lambda l, j: (l, 0, j)),
            pl.BlockSpec((1, 1, tn), lambda l, j: (l, 0, j)),
        ],
        out_specs=pl.BlockSpec((1, rows, tn), lambda l, j: (l, 0, j)),
        out_shape=jax.ShapeDtypeStruct((depth, rows, n), F32),
        compiler_params=_cparams(("arbitrary", "arbitrary")),
        name="mod_vectors",
    )(cc, mod_w, mod_b.reshape(depth, 1, n))


def _ffn_kernel(h_ref, sh_ref, sc_ref, gt_ref, gpre_ref, gpost_ref, wg_ref, wu_ref, wd_ref,
                o_ref, u_ref, acc_ref, r_ref, *, nk):
    k = pl.program_id(1)

    @pl.when(k == 0)
    def _():
        _norm_modulate_rows(h_ref, u_ref, r_ref, gpre_ref[...] * (1.0 + sc_ref[0]), sh_ref[0])
        acc_ref[...] = jnp.zeros_like(acc_ref)

    u = u_ref[...]
    g = jnp.dot(u, wg_ref[...], preferred_element_type=F32)
    up = jnp.dot(u, wu_ref[...], preferred_element_type=F32)
    a = (_silu(g) * up).astype(BF16)
    acc_ref[...] += jnp.dot(a, wd_ref[...], preferred_element_type=F32)

    @pl.when(k == nk - 1)
    def _():
        _norm_residual_rows(acc_ref, h_ref, o_ref, r_ref, FFN_RES * gt_ref[0] * gpost_ref[...])


def _ffn(h, shift, scale, gate, g_pre, g_post, wg, wu, wd, widx, *, tm=512, tf=512):
    m, d = h.shape
    f = wg.shape[-1]
    wl, wh = widx
    bm = shift.shape[0]
    rows_per_b = m // bm
    tm = _tile(rows_per_b, tm)
    tf = _tile(f, tf)
    nk = f // tf
    per_b = rows_per_b // tm
    mod_spec = pl.BlockSpec((1, 1, d), lambda i, k: (i // per_b, 0, 0))
    vec_spec = pl.BlockSpec((1, d), lambda i, k: (0, 0))
    return pl.pallas_call(
        functools.partial(_ffn_kernel, nk=nk),
        grid=(m // tm, nk),
        in_specs=[
            pl.BlockSpec((tm, d), lambda i, k: (i, 0)),
            mod_spec, mod_spec, mod_spec, vec_spec, vec_spec,
            pl.BlockSpec((None, None, d, tf), lambda i, k: (wl, wh, 0, k)),
            pl.BlockSpec((None, None, d, tf), lambda i, k: (wl, wh, 0, k)),
            pl.BlockSpec((None, None, tf, d), lambda i, k: (wl, wh, k, 0)),
        ],
        out_specs=pl.BlockSpec((tm, d), lambda i, k: (i, 0)),
        out_shape=jax.ShapeDtypeStruct((m, d), F32),
        scratch_shapes=[pltpu.VMEM((tm, d), BF16), pltpu.VMEM((tm, d), F32), pltpu.VMEM((tm, LANES), F32)],
        compiler_params=_cparams(("parallel", "arbitrary")),
        name="ffn_half_step",
    )(h, shift, scale, gate, g_pre, g_post, wg, wu, wd)


def _proj_kernel(*refs, modulated):
    if modulated:
        x_ref, g_ref, sh_ref, sc_ref, w_ref, o_ref, u_ref, r_ref = refs
    else:
        x_ref, g_ref, w_ref, o_ref, u_ref, r_ref = refs

    @pl.when(pl.program_id(1) == 0)
    def _():
        if modulated:
            _norm_modulate_rows(x_ref, u_ref, r_ref, g_ref[...] * (1.0 + sc_ref[0]), sh_ref[0])
        else:
            _norm_modulate_rows(x_ref, u_ref, r_ref, g_ref[...], None)

    o_ref[...] = jnp.dot(u_ref[...], w_ref[...], preferred_element_type=F32).astype(o_ref.dtype)


def _proj(x, gain, w, *, out_dtype, shift=None, scale=None, kblock=0, tm=1024):
    m = x.shape[0]
    k, n = w.shape
    modulated = shift is not None
    bm = shift.shape[0] if modulated else 1
    rows_per_b = m // bm
    tm = _tile(rows_per_b, tm)
    per_b = rows_per_b // tm
    tn = n if k * n * 2 <= 6 * 1024 * 1024 else _tile(n, 1024)
    in_specs = [pl.BlockSpec((tm, k), lambda i, j: (i, kblock)),
                pl.BlockSpec((1, k), lambda i, j: (0, 0))]
    args = [x, gain]
    if modulated:
        mod_spec = pl.BlockSpec((1, 1, k), lambda i, j: (i // per_b, 0, 0))
        in_specs += [mod_spec, mod_spec]
        args += [shift, scale]
    in_specs.append(pl.BlockSpec((k, tn), lambda i, j: (0, j)))
    args.append(w)
    return pl.pallas_call(
        functools.partial(_proj_kernel, modulated=modulated),
        grid=(m // tm, n // tn),
        in_specs=in_specs,
        out_specs=pl.BlockSpec((tm, tn), lambda i, j: (i, j)),
        out_shape=jax.ShapeDtypeStruct((m, n), out_dtype),
        scratch_shapes=[pltpu.VMEM((tm, k), BF16), pltpu.VMEM((tm, LANES), F32)],
        compiler_params=_cparams(("parallel", "arbitrary")),
        name="norm_proj",
    )(*args)


def _out_proj_kernel(a_ref, w_ref, gpost_ref, gt_ref, h_ref, o_ref, acc_ref, r_ref, *, nk):
    if nk == 1:
        y = jnp.dot(a_ref[...], w_ref[...], preferred_element_type=F32)
        o_ref[...] = h_ref[...] + _rms(y) * (gt_ref[0] * gpost_ref[...])
        return
    k = pl.program_id(1)

    @pl.when(k == 0)
    def _():
        acc_ref[...] = jnp.zeros_like(acc_ref)

    acc_ref[...] += jnp.dot(a_ref[...], w_ref[...], preferred_element_type=F32)

    @pl.when(k == nk - 1)
    def _():
        _norm_residual_rows(acc_ref, h_ref, o_ref, r_ref, gt_ref[0] * gpost_ref[...])


def _out_proj(a, w, g_post, gate, h, *, tm=512, tk=2048):
    m, kdim = a.shape
    d = w.shape[1]
    bm = gate.shape[0]
    rows_per_b = m // bm
    tm = _tile(rows_per_b, tm)
    per_b = rows_per_b // tm
    tk = _tile(kdim, tk)
    nk = kdim // tk
    return pl.pallas_call(
        functools.partial(_out_proj_kernel, nk=nk),
        grid=(m // tm, nk),
        in_specs=[
            pl.BlockSpec((tm, tk), lambda i, k: (i, k)),
            pl.BlockSpec((tk, d), lambda i, k: (k, 0)),
            pl.BlockSpec((1, d), lambda i, k: (0, 0)),
            pl.BlockSpec((1, 1, d), lambda i, k: (i // per_b, 0, 0)),
            pl.BlockSpec((tm, d), lambda i, k: (i, 0)),
        ],
        out_specs=pl.BlockSpec((tm, d), lambda i, k: (i, 0)),
        out_shape=jax.ShapeDtypeStruct((m, d), F32),
        scratch_shapes=[pltpu.VMEM((tm, d), F32), pltpu.VMEM((tm, LANES), F32)],
        compiler_params=_cparams(("parallel", "arbitrary")),
        name="out_proj",
    )(a, w, g_post, gate, h)


def _rope128(x, c_ref, s_ref):
    return x * c_ref[...] + pltpu.roll(x, 64, axis=1) * s_ref[...]


def _prep_kernel(*refs, mode, rope, norm, scale, heads, xcol, x2col):
    refs = list(refs)
    o_ref = refs.pop()
    x_ref = refs.pop(0)
    x2_ref = refs.pop(0) if mode == "mla_k" else None
    g_ref = refs.pop(0) if norm else None
    c_ref, s_ref = (refs.pop(0), refs.pop(0)) if rope else (None, None)
    wx = HEAD_DIM if mode in ("head", "mla_k") else 2 * HEAD_DIM
    wo = HEAD_DIM if mode == "head" else 2 * HEAD_DIM

    if mode == "mla_k":
        shared = x2_ref[0, :, pl.ds(x2col * LANES, LANES)].astype(F32)
        if rope:
            shared = _rope128(shared, c_ref, s_ref)
        shared = shared.astype(o_ref.dtype)
    for h in range(heads):
        x = x_ref[0, :, pl.ds(xcol(h) * wx, wx)].astype(F32)
        if mode == "head":
            if norm:
                x = _rms(x) * g_ref[...]
            if rope:
                x = _rope128(x, c_ref, s_ref)
            o_ref[0, :, pl.ds(h * wo, wo)] = (x * scale).astype(o_ref.dtype)
        elif mode == "mla_q":
            r = x[:, HEAD_DIM:]
            if rope:
                r = _rope128(r, c_ref, s_ref)
            o_ref[0, :, pl.ds(h * wo, HEAD_DIM)] = (x[:, :HEAD_DIM] * scale).astype(o_ref.dtype)
            o_ref[0, :, pl.ds(h * wo + HEAD_DIM, HEAD_DIM)] = (r * scale).astype(o_ref.dtype)
        else:
            o_ref[0, :, pl.ds(h * wo, HEAD_DIM)] = x.astype(o_ref.dtype)
            o_ref[0, :, pl.ds(h * wo + HEAD_DIM, HEAD_DIM)] = shared


def _prep(x, *, mode, heads, xcol, x2=None, x2col=None, gain=None, tables=None, scale=1.0, ts=256):
    b, t, _ = x.shape
    ts = _tile(t, ts)
    wo = HEAD_DIM if mode == "head" else 2 * HEAD_DIM
    in_specs = [pl.BlockSpec((1, ts, x.shape[2]), lambda bb, i: (bb, i, 0))]
    args = [x]
    if mode == "mla_k":
        in_specs.append(pl.BlockSpec((1, ts, x2.shape[2]), lambda bb, i: (bb, i, 0)))
        args.append(x2)
    if gain is not None:
        in_specs.append(pl.BlockSpec((1, HEAD_DIM), lambda bb, i: (0, 0)))
        args.append(gain)
    if tables is not None:
        tspec = pl.BlockSpec((ts, LANES), lambda bb, i: (i, 0))
        in_specs += [tspec, tspec]
        args += list(tables)
    return pl.pallas_call(
        functools.partial(_prep_kernel, mode=mode, rope=tables is not None, norm=gain is not None,
                          scale=scale, heads=heads, xcol=xcol, x2col=x2col),
        grid=(b, t // ts),
        in_specs=in_specs,
        out_specs=pl.BlockSpec((1, ts, heads * wo), lambda bb, i: (bb, i, 0)),
        out_shape=jax.ShapeDtypeStruct((b, t, heads * wo), BF16),
        compiler_params=_cparams(("parallel", "parallel")),
        name="head_prep_" + mode,
    )(*args)


def _flash_kernel(*refs, nk, scale, has_lat, n_sub):
    if has_lat:
        q_ref, kc_ref, vc_ref, kl_ref, vl_ref, o_ref, m_ref, acc_ref, va_ref = refs
    else:
        q_ref, kc_ref, vc_ref, o_ref, m_ref, acc_ref, va_ref = refs
    kk = pl.program_id(3)
    tq = q_ref.shape[1]
    dv = o_ref.shape[2]
    tr = tq // n_sub

    def step(k, v):
        tk = k.shape[0]
        va_ref[pl.ds(0, tk), pl.ds(0, dv)] = v
        va = va_ref[pl.ds(0, tk), :]

        def scores(r):
            q = q_ref[0, pl.ds(r * tr, tr), :]
            if scale != 1.0:
                q = (q.astype(F32) * scale).astype(BF16)
            return lax.dot_general(q, k, (((1,), (1,)), ((), ())), preferred_element_type=F32)

        s_next = scores(0)
        for r in range(n_sub):
            rows = pl.ds(r * tr, tr)
            s = s_next
            if r + 1 < n_sub:
                s_next = scores(r + 1)
            m_prev = m_ref[rows, :]
            m_new = jnp.maximum(m_prev, jnp.max(s, axis=-1, keepdims=True))
            alpha = jnp.exp2(m_prev - m_new)
            p = jnp.exp2((s - _lane_tile(m_new, tk // LANES)).astype(BF16))
            acc_ref[rows, :] = (_lane_tile(alpha, 2 * dv // LANES) * acc_ref[rows, :]
                                + jnp.dot(p, va, preferred_element_type=F32))
            m_ref[rows, :] = m_new

    @pl.when(kk == 0)
    def _():
        m_ref[...] = jnp.full_like(m_ref, NEG_INF)
        acc_ref[...] = jnp.zeros_like(acc_ref)
        va_ref[:, pl.ds(dv, dv)] = jnp.ones((va_ref.shape[0], dv), va_ref.dtype)
        step(kc_ref[0], vc_ref[0])

    if has_lat:
        @pl.when(kk > 0)
        def _():
            step(kl_ref[0], vl_ref[0])

    @pl.when(kk == nk - 1)
    def _():
        o_ref[0] = (acc_ref[:, pl.ds(0, dv)] / acc_ref[:, pl.ds(dv, dv)]).astype(o_ref.dtype)


def _flash(q, kc, vc, kl=None, vl=None, *, heads, dq, dv, qcol, kcol, vcol, scale=1.0,
           tq=8192, tk=2048, tr=512):
    b, s, _ = q.shape
    tc = kc.shape[1]
    tq = _tile(s, tq)
    has_lat = kl is not None
    if has_lat:
        tk = _tile(kl.shape[1], tk)
        nk = 1 + kl.shape[1] // tk
    else:
        nk = 1
    in_specs = [
        pl.BlockSpec((1, tq, dq), lambda bb, h, i, kk: (bb, i, qcol(h))),
        pl.BlockSpec((1, tc, dq), lambda bb, h, i, kk: (bb, 0, kcol(h))),
        pl.BlockSpec((1, tc, dv), lambda bb, h, i, kk: (bb, 0, vcol(h))),
    ]
    args = [q, kc, vc]
    if has_lat:
        in_specs += [
            pl.BlockSpec((1, tk, dq), lambda bb, h, i, kk: (bb, jnp.maximum(kk - 1, 0), kcol(h))),
            pl.BlockSpec((1, tk, dv), lambda bb, h, i, kk: (bb, jnp.maximum(kk - 1, 0), vcol(h))),
        ]
        args += [kl, vl]
    return pl.pallas_call(
        functools.partial(_flash_kernel, nk=nk, scale=scale, has_lat=has_lat, n_sub=tq // _tile(tq, tr)),
        grid=(b, heads, s // tq, nk),
        in_specs=in_specs,
        out_specs=pl.BlockSpec((1, tq, dv), lambda bb, h, i, kk: (bb, i, h)),
        out_shape=jax.ShapeDtypeStruct((b, s, heads * dv), BF16),
        scratch_shapes=[pltpu.VMEM((tq, LANES), F32), pltpu.VMEM((tq, 2 * dv), F32),
                        pltpu.VMEM((max(tc, tk), 2 * dv), BF16)],
        compiler_params=_cparams(("parallel", "parallel", "parallel", "arbitrary")),
        name="flash_attention",
    )(*args)


def _na_kernel(q_ref, k_ref, v_ref, kc_ref, vc_ref, bias_ref, o_ref, *, band, seq, scale, n_sub):
    i = pl.program_id(2)
    tq = q_ref.shape[1]
    start = jnp.clip(i * tq - (band - tq) // 2, 0, seq - band)
    start = pl.multiple_of(start, 256)
    dv = o_ref.shape[2]
    kb = k_ref[0, pl.ds(start, band), :]
    kc = kc_ref[0]
    vb = jnp.concatenate([v_ref[0, pl.ds(start, band), :], jnp.ones((band, dv), BF16)], axis=1)
    vc = jnp.concatenate([vc_ref[0], jnp.ones((kc.shape[0], dv), BF16)], axis=1)
    dn = (((1,), (1,)), ((), ()))
    tr = tq // n_sub
    for r in range(n_sub):
        rows = pl.ds(r * tr, tr)
        q = (q_ref[0, rows, :].astype(F32) * scale).astype(BF16)
        s_nb = lax.dot_general(q, kb, dn, preferred_element_type=F32) + bias_ref[0, 0, rows, :]
        s_cx = lax.dot_general(q, kc, dn, preferred_element_type=F32)
        m = jnp.maximum(jnp.max(s_nb, axis=-1, keepdims=True), jnp.max(s_cx, axis=-1, keepdims=True))
        p_nb = jnp.exp2((s_nb - m).astype(BF16))
        p_cx = jnp.exp2((s_cx - m).astype(BF16))
        o = (jnp.dot(p_nb, vb, preferred_element_type=F32) + jnp.dot(p_cx, vc, preferred_element_type=F32))
        o_ref[0, rows, :] = (o[:, :dv] / o[:, dv:]).astype(o_ref.dtype)


def _na_bias_tiles(rpb, rows):
    nr, wr, wc, w = NA_QROWS, NA_WIN_R, NA_WIN_C, GRID_W
    brows = nr + wr
    heads = rpb.shape[0]
    lpad = w - wc
    vp = jnp.pad(rpb.astype(F32) * LOG2_E, ((0, 0), (0, 0), (lpad, lpad)), constant_values=NEG_INF)
    toep = jnp.stack([vp[:, :, w - 1 - c: 2 * w - 1 - c] for c in range(w)], axis=2)
    col = jnp.arange(w, dtype=jnp.int32)
    c0 = jnp.clip(col - wc // 2, 0, w - wc)
    in_col = (col[None, :] >= c0[:, None]) & (col[None, :] < c0[:, None] + wc)
    toep = jnp.where(in_col[None, None], toep, NEG_INF)
    masked = jnp.full((heads, 1, w, w), NEG_INF, F32)
    tiles = []
    for r_first, b_first in ((0, 0), (nr, nr - wr // 2), (rows - nr, rows - brows)):
        per_row = []
        for a in range(nr):
            r = r_first + a
            r0 = min(max(r - wr // 2, 0), rows - wr)
            lo = r0 - b_first
            dr_lo = r0 - r + (wr - 1)
            blk = toep[:, dr_lo: dr_lo + wr]
            blk = jnp.concatenate([jnp.tile(masked, (1, lo, 1, 1)), blk,
                                   jnp.tile(masked, (1, brows - wr - lo, 1, 1))], axis=1)
            per_row.append(jnp.swapaxes(blk, 1, 2))
        tiles.append(jnp.stack(per_row, axis=1).reshape(heads, nr * w, brows * w))
    return jnp.stack(tiles, axis=1)


def _na_attention(z, zc, rpb, *, heads):
    b, s, _ = z.shape
    tcx = zc.shape[1]
    rows = s // GRID_W
    tq = NA_QROWS * GRID_W
    band = (NA_QROWS + NA_WIN_R) * GRID_W
    nblk = s // tq
    bias = _na_bias_tiles(rpb, rows)
    scale = HEAD_DIM ** -0.5 * LOG2_E

    def variant(i):
        return jnp.where(i == 0, 0, jnp.where(i == nblk - 1, 2, 1))

    return pl.pallas_call(
        functools.partial(_na_kernel, band=band, seq=s, scale=scale, n_sub=4),
        grid=(b, heads, nblk),
        in_specs=[
            pl.BlockSpec((1, tq, HEAD_DIM), lambda bb, h, i: (bb, i, h)),
            pl.BlockSpec((1, s, HEAD_DIM), lambda bb, h, i: (bb, 0, heads + h)),
            pl.BlockSpec((1, s, HEAD_DIM), lambda bb, h, i: (bb, 0, 2 * heads + h)),
            pl.BlockSpec((1, tcx, HEAD_DIM), lambda bb, h, i: (bb, 0, heads + h)),
            pl.BlockSpec((1, tcx, HEAD_DIM), lambda bb, h, i: (bb, 0, 2 * heads + h)),
            pl.BlockSpec((1, 1, tq, band), lambda bb, h, i: (h, variant(i), 0, 0)),
        ],
        out_specs=pl.BlockSpec((1, tq, HEAD_DIM), lambda bb, h, i: (bb, i, h)),
        out_shape=jax.ShapeDtypeStruct((b, s, heads * HEAD_DIM), BF16),
        compiler_params=_cparams(("parallel", "parallel", "arbitrary")),
        name="neighbourhood_attention",
    )(z, z, z, zc, zc, bias)


def _ret_kernel(dec_ref, *refs, nc):
    ins, (of_ref, ob_ref, st_ref, din_ref) = refs[:14], refs[14:]
    dirs = (ins[:7] + (of_ref,), ins[7:] + (ob_ref,))
    h = pl.program_id(1)
    t = pl.program_id(2)
    half = RET_QK // 2
    lgs = [jnp.log1p(-jnp.exp2(jnp.full((1, 1), dec_ref[d, h], F32))) for d in (0, 1)]

    def rope(x, cos_ref, sin_ref):
        c, s = cos_ref[...], sin_ref[...]
        x1, x2 = x[:, :half], x[:, half:]
        return jnp.concatenate([x1 * c - x2 * s, x2 * c + x1 * s], axis=-1)

    def update(d, k, v):
        n = k.shape[0]
        pos = lax.broadcasted_iota(jnp.int32, (n, 1), 0).astype(F32)
        zeta = jnp.exp((n - 1.0 - pos if d == 0 else pos) * lgs[d])
        kz = (k * zeta).astype(BF16)
        kv = lax.dot_general(kz, v, (((0,), (0,)), ((), ())), preferred_element_type=F32)
        st_ref[d] = st_ref[d] * jnp.exp(n * lgs[d]) + kv

    @pl.when(t == 0)
    def _():
        st_ref[...] = jnp.zeros_like(st_ref)
        n = din_ref.shape[1]
        ri = lax.broadcasted_iota(jnp.int32, (n, n), 0)
        ci = lax.broadcasted_iota(jnp.int32, (n, n), 1)
        for d in (0, 1):
            dist = ri - ci if d == 0 else ci - ri
            din_ref[d] = jnp.where(dist >= 0, jnp.exp(jnp.maximum(dist, 0).astype(F32) * lgs[d]), 0.0)

    @pl.when(t < nc)
    def _():
        for d in (0, 1):
            kc_ref, vc_ref = dirs[d][3], dirs[d][4]
            update(d, kc_ref[0].astype(F32) * RET_QK ** -0.5, vc_ref[0])

    @pl.when(t >= nc)
    def _():
        for d in (0, 1):
            q_ref, k_ref, v_ref, _, _, cos_ref, sin_ref, o_ref = dirs[d]
            q = rope(q_ref[0].astype(F32), cos_ref, sin_ref)
            k = rope(k_ref[0].astype(F32) * RET_QK ** -0.5, cos_ref, sin_ref)
            v = v_ref[0]
            n = q.shape[0]
            pos = lax.broadcasted_iota(jnp.int32, (n, 1), 0).astype(F32)
            qb = q.astype(BF16)
            s = lax.dot_general(qb, k.astype(BF16), (((1,), (1,)), ((), ())),
                                preferred_element_type=F32) * din_ref[d]
            xi = jnp.exp((pos + 1.0 if d == 0 else n - pos) * lgs[d])
            o = (jnp.dot(s.astype(BF16), v, preferred_element_type=F32)
                 + jnp.dot(qb, st_ref[d].astype(BF16), preferred_element_type=F32) * xi)
            o_ref[0] = o.astype(o_ref.dtype)
            update(d, k, v)


def _retention(z, zc, decay_log2, cos, sin, *, heads):
    b, s, _ = z.shape
    tcx = zc.shape[1]
    chunk = _tile(s, RET_CHUNK)
    cchunk = _tile(tcx, RET_CHUNK)
    nc, nl = tcx // cchunk, s // chunk
    kcol0 = heads
    vcol0 = (2 * heads * RET_QK) // RET_V

    def cidx(d, t):
        c = jnp.minimum(t, nc - 1)
        return c if d == 0 else nc - 1 - c

    def lidx(d, t):
        c = jnp.maximum(t - nc, 0)
        return c if d == 0 else nl - 1 - c

    in_specs = [pl.BlockSpec(memory_space=pltpu.SMEM)]
    out_specs = []
    for d in (0, 1):
        in_specs += [
            pl.BlockSpec((1, chunk, RET_QK), lambda bb, h, t, d=d: (bb, lidx(d, t), h)),
            pl.BlockSpec((1, chunk, RET_QK), lambda bb, h, t, d=d: (bb, lidx(d, t), kcol0 + h)),
            pl.BlockSpec((1, chunk, RET_V), lambda bb, h, t, d=d: (bb, lidx(d, t), vcol0 + h)),
            pl.BlockSpec((1, cchunk, RET_QK), lambda bb, h, t, d=d: (bb, cidx(d, t), kcol0 + h)),
            pl.BlockSpec((1, cchunk, RET_V), lambda bb, h, t, d=d: (bb, cidx(d, t), vcol0 + h)),
            pl.BlockSpec((chunk, RET_QK // 2), lambda bb, h, t, d=d: (lidx(d, t), 0)),
            pl.BlockSpec((chunk, RET_QK // 2), lambda bb, h, t, d=d: (lidx(d, t), 0)),
        ]
        out_specs.append(pl.BlockSpec((1, chunk, RET_V), lambda bb, h, t, d=d: (bb, lidx(d, t), h)))
    out = jax.ShapeDtypeStruct((b, s, heads * RET_V), BF16)
    return pl.pallas_call(
        functools.partial(_ret_kernel, nc=nc),
        grid=(b, heads, nc + nl),
        in_specs=in_specs,
        out_specs=out_specs,
        out_shape=[out, out],
        scratch_shapes=[pltpu.VMEM((2, RET_QK, RET_V), F32), pltpu.VMEM((2, chunk, chunk), F32)],
        compiler_params=_cparams(("parallel", "parallel", "arbitrary")),
        name="retention_scan",
    )(decay_log2, *([z, z, z, zc, zc, cos, sin] * 2))


def _ret_gate_kernel(of_ref, ob_ref, g_ref, y_ref, *, heads):
    for h in range(heads):
        cols = pl.ds(h * RET_V, RET_V)
        o = of_ref[0, :, cols].astype(F32) + ob_ref[0, :, cols].astype(F32)
        y_ref[0, :, cols] = (_silu(g_ref[0, :, cols].astype(F32)) * _rms(o)).astype(y_ref.dtype)


def _ret_gate(o_fwd, o_bwd, z, *, heads, ts=256):
    b, s, width = o_fwd.shape
    ts = _tile(s, ts)
    gcol = (2 * heads * RET_QK + heads * RET_V) // width
    spec = pl.BlockSpec((1, ts, width), lambda bb, i: (bb, i, 0))
    return pl.pallas_call(
        functools.partial(_ret_gate_kernel, heads=heads),
        grid=(b, s // ts),
        in_specs=[spec, spec, pl.BlockSpec((1, ts, width), lambda bb, i: (bb, i, gcol))],
        out_specs=spec,
        out_shape=jax.ShapeDtypeStruct((b, s, width), BF16),
        compiler_params=_cparams(("parallel", "parallel")),
        name="retention_gate",
    )(o_fwd, o_bwd, z)


def _axial_angles(n_tokens, rot_dim):
    t = jnp.arange(n_tokens, dtype=jnp.int32)
    row = (t // GRID_W).astype(F32)
    col = (t % GRID_W).astype(F32)
    n_f = rot_dim // 4
    freqs = ROPE_BASE ** (-jnp.arange(n_f, dtype=F32) / n_f)
    return jnp.concatenate([row[:, None] * freqs, col[:, None] * freqs], axis=-1)


def _rope_slot_tables(ang):
    pad = jnp.zeros((ang.shape[0], 64 - ang.shape[1]), F32)
    cos, sin = jnp.cos(ang), jnp.sin(ang)
    return (jnp.concatenate([cos, pad, cos, pad], axis=-1),
            jnp.concatenate([-sin, pad, sin, pad], axis=-1))


def _spread_rope_cols(w):
    half = MLA_ROPE // 2
    z = jnp.zeros(w.shape[:-1] + (64 - half,), w.dtype)
    return jnp.concatenate([w[..., :half], z, w[..., half:], z], axis=-1)


def _mla_mixer(in_proj, b, s, tcx, heads, w_in, q_gain, kv_gain, w_uq, w_ukv):
    rq = q_gain.shape[-1]
    w_in = jnp.concatenate([w_in[:, :2 * rq], _spread_rope_cols(w_in[:, 2 * rq:])], axis=-1).astype(BF16)
    w_uq = w_uq.reshape(rq, heads, HEAD_DIM + MLA_ROPE)
    w_uq = jnp.concatenate([w_uq[..., :HEAD_DIM], _spread_rope_cols(w_uq[..., HEAD_DIM:])],
                           axis=-1).reshape(rq, heads * 2 * HEAD_DIM).astype(BF16)
    w_ukv = w_ukv.astype(BF16)
    tables = _rope_slot_tables(_axial_angles(s, MLA_ROPE))
    scale = (HEAD_DIM + MLA_ROPE) ** -0.5 * LOG2_E
    zl, zc = in_proj(w_in, F32)
    ropecol = 2 * rq // LANES

    def qkv(z, n_tok, tabs):
        q = _proj(z, q_gain[None], w_uq, kblock=0, out_dtype=BF16).reshape(b, n_tok, -1)
        kv = _proj(z, kv_gain[None], w_ukv, kblock=1, out_dtype=BF16).reshape(b, n_tok, -1)
        q = _prep(q, mode="mla_q", heads=heads, xcol=lambda hh: hh, tables=tabs, scale=scale)
        k = _prep(kv, mode="mla_k", heads=heads, xcol=lambda hh: 2 * hh,
                  x2=z.reshape(b, n_tok, -1), x2col=ropecol, tables=tabs)
        return q, k, kv

    ql, kl, kvl = qkv(zl, s, tables)
    qc, kc, kvc = qkv(zc, tcx, None)
    cols = dict(heads=heads, dq=2 * HEAD_DIM, dv=HEAD_DIM, qcol=lambda hh: hh, kcol=lambda hh: hh,
                vcol=lambda hh: 2 * hh + 1)
    return _flash(ql, kc, kvc, kl, kvl, **cols), _flash(qc, kc, kvc, **cols)


def _gqa_mixer(in_proj, b, s, tcx, heads, w_in, q_gain, k_gain):
    hk = (w_in.shape[-1] // HEAD_DIM - heads) // 2
    grp = heads // hk
    tables = _rope_slot_tables(_axial_angles(s, HEAD_DIM))
    zl, zc = in_proj(w_in.astype(BF16), BF16)
    zl = zl.reshape(b, s, -1)
    zc = zc.reshape(b, tcx, -1)
    scale = HEAD_DIM ** -0.5 * LOG2_E
    qg, kg = q_gain[None], k_gain[None]
    ql = _prep(zl, mode="head", heads=heads, xcol=lambda hh: hh, gain=qg, tables=tables, scale=scale)
    kl = _prep(zl, mode="head", heads=hk, xcol=lambda hh: heads + hh, gain=kg, tables=tables)
    qc = _prep(zc, mode="head", heads=heads, xcol=lambda hh: hh, gain=qg, scale=scale)
    kc = _prep(zc, mode="head", heads=hk, xcol=lambda hh: heads + hh, gain=kg)
    cols = dict(heads=heads, dq=HEAD_DIM, dv=HEAD_DIM, qcol=lambda hh: hh,
                kcol=lambda hh: hh // grp, vcol=lambda hh: heads + hk + hh // grp)
    return _flash(ql, kc, zc, kl, zl, **cols), _flash(qc, kc, zc, **cols)


def _na_mixer(in_proj, b, s, tcx, heads, w_in, rpb):
    zl, zc = in_proj(w_in.astype(BF16), BF16)
    zl = zl.reshape(b, s, -1)
    zc = zc.reshape(b, tcx, -1)
    y = _na_attention(zl, zc, rpb, heads=heads)
    yc = _flash(zc, zc, zc, heads=heads, dq=HEAD_DIM, dv=HEAD_DIM, qcol=lambda hh: hh,
                kcol=lambda hh: heads + hh, vcol=lambda hh: 2 * heads + hh,
                scale=HEAD_DIM ** -0.5 * LOG2_E)
    return y, yc


def _ret_mixer(in_proj, b, s, tcx, w_in, decay_log2):
    rh = decay_log2.shape[-1]
    pos = jnp.arange(s, dtype=F32)
    freqs = ROPE_BASE ** (-jnp.arange(RET_QK // 2, dtype=F32) / (RET_QK // 2))
    ang = pos[:, None] * freqs
    zl, zc = in_proj(w_in.astype(BF16), BF16)
    zl = zl.reshape(b, s, -1)
    zc = zc.reshape(b, tcx, -1)
    o_fwd, o_bwd = _retention(zl, zc, decay_log2.astype(F32), jnp.cos(ang), jnp.sin(ang), heads=rh)
    return _ret_gate(o_fwd, o_bwd, zl, heads=rh)


def kernel(x, c, ctx, c_ctx, mod_w, mod_b, norm_g, ffn_w_gate, ffn_w_up, ffn_w_down,
           mla_w_in, mla_q_gain, mla_kv_gain, mla_w_uq, mla_w_ukv, mla_w_o,
           gqa_w_in, gqa_q_gain, gqa_k_gain, gqa_w_o,
           na_w_in, na_rpb, na_w_o,
           ret_w_in, ret_decay_log2, ret_w_o):
    b, s, d = x.shape
    tcx = ctx.shape[1]
    depth = mod_w.shape[0]
    assert depth == 4 and s % (NA_QROWS * GRID_W) == 0 and s // GRID_W >= NA_QROWS + NA_WIN_R
    heads = d // HEAD_DIM

    cc = jnp.concatenate([c, c_ctx[None], jnp.zeros((8 - b - 1, d), F32)], axis=0)
    mods = _mod_vectors(cc, mod_w, mod_b).reshape(depth, 8, N_MOD, d)

    h = x.reshape(b * s, d)
    hc = ctx.reshape(b * tcx, d)
    w_gate, w_up, w_down = ffn_w_gate.astype(BF16), ffn_w_up.astype(BF16), ffn_w_down.astype(BF16)

    for i in range(depth):
        kind = i % 4
        last = i == depth - 1
        ml = [mods[i, :b, j][:, None, :] for j in range(N_MOD)]
        mc = [mods[i, b:b + 1, j][:, None, :] for j in range(N_MOD)]
        gn = [norm_g[i, j][None, :] for j in range(6)]

        def ffn(hh, m, f):
            return _ffn(hh, m[3 * f], m[3 * f + 1], m[3 * f + 2], gn[2 * f], gn[2 * f + 1],
                        w_gate, w_up, w_down, (i, f // 2))

        h = ffn(h, ml, 0)
        hc = ffn(hc, mc, 0)

        def in_proj(w, out_dtype):
            zl = _proj(h, gn[2], w, shift=ml[3], scale=ml[4], out_dtype=out_dtype)
            zc = _proj(hc, gn[2], w, shift=mc[3], scale=mc[4], out_dtype=out_dtype)
            return zl, zc

        if kind == 0:
            y, yc = _mla_mixer(in_proj, b, s, tcx, heads, mla_w_in[0], mla_q_gain[0], mla_kv_gain[0],
                               mla_w_uq[0], mla_w_ukv[0])
            w_o = mla_w_o[0]
        elif kind == 1:
            y, yc = _gqa_mixer(in_proj, b, s, tcx, heads, gqa_w_in[0], gqa_q_gain[0], gqa_k_gain[0])
            w_o = gqa_w_o[0]
        elif kind == 2:
            y, yc = _na_mixer(in_proj, b, s, tcx, heads, na_w_in[0], na_rpb[0])
            w_o = na_w_o[0]
        else:
            y, yc = _ret_mixer(in_proj, b, s, tcx, ret_w_in[0], ret_decay_log2[0]), None
            w_o = ret_w_o[0]

        w_o = w_o.astype(BF16)
        h = _out_proj(y.reshape(b * s, -1), w_o, gn[3], ml[5], h)
        h = ffn(h, ml, 2)
        if not last:
            hc = _out_proj(yc.reshape(b * tcx, -1), w_o, gn[3], mc[5], hc)
            hc = ffn(hc, mc, 2)
    return h.reshape(b, s, d)
```

```python
import functools

import jax
import jax.numpy as jnp
from jax import lax
from jax.experimental import pallas as pl
from jax.experimental.pallas import tpu as pltpu

F32 = jnp.float32
BF16 = jnp.bfloat16

RMS_EPS = 1e-6
ROPE_BASE = 10000.0
FFN_RES = 0.5
GRID_W = 64
N_MOD = 9
LANES = 128
ROW_CHUNK = 64
APPLY_CHUNK = 32
HEAD_DIM = 128
MLA_ROPE = 64
RET_QK = 256
RET_V = 512
RET_CHUNK = 512
NA_WIN_R = 8
NA_WIN_C = 16
NA_QROWS = 8
NEG_INF = -1e30
LOG2_E = 1.4426950408889634
VMEM_LIMIT_BYTES = 56 * 1024 * 1024


def _cparams(semantics):
    return pltpu.CompilerParams(dimension_semantics=semantics, vmem_limit_bytes=VMEM_LIMIT_BYTES)


def _tile(n, pref):
    if n <= pref:
        return n
    t = pref
    while n % t:
        t //= 2
    return t


def _rms(x):
    return x * lax.rsqrt(jnp.mean(x * x, axis=-1, keepdims=True) + RMS_EPS)


def _silu(x):
    return x * jax.nn.sigmoid(x)


def _row_chunks(n_rows):
    step = min(ROW_CHUNK, n_rows)
    return [pl.ds(r, step) for r in range(0, n_rows, step)]


def _row_rsqrt(x_ref, r_ref):
    for rows in _row_chunks(x_ref.shape[0]):
        x = x_ref[rows, :].astype(F32)
        ms = jnp.mean(x * x, axis=-1, keepdims=True)
        r_ref[rows, :] = jnp.broadcast_to(lax.rsqrt(ms + RMS_EPS), (rows.size, LANES))


def _loop_row_chunks(n_rows, body):
    step = min(APPLY_CHUNK, n_rows)

    def it(c, carry):
        body(pl.ds(pl.multiple_of(c * step, step), step))
        return carry

    lax.fori_loop(0, n_rows // step, it, 0, unroll=2)


def _norm_modulate_rows(x_ref, u_ref, r_ref, gain, shift):
    _row_rsqrt(x_ref, r_ref)

    def apply(rows):
        u = x_ref[rows, :].astype(F32) * _lane_tile(r_ref[rows, :], x_ref.shape[1] // LANES) * gain
        u_ref[rows, :] = (u if shift is None else u + shift).astype(u_ref.dtype)

    _loop_row_chunks(x_ref.shape[0], apply)


def _norm_residual_rows(y_ref, h_ref, o_ref, r_ref, gain):
    _row_rsqrt(y_ref, r_ref)

    def apply(rows):
        y = y_ref[rows, :] * _lane_tile(r_ref[rows, :], y_ref.shape[1] // LANES)
        o_ref[rows, :] = h_ref[rows, :] + y * gain

    _loop_row_chunks(y_ref.shape[0], apply)


def _lane_tile(x, n):
    return x if n == 1 else jnp.concatenate([x] * n, axis=1)


def _mod_kernel(cc_ref, w_ref, b_ref, o_ref):
    s = _silu(cc_ref[...]).astype(BF16)
    o_ref[0] = jnp.dot(s, w_ref[0].astype(BF16), preferred_element_type=F32) + b_ref[0]


def _mod_vectors(cc, mod_w, mod_b):
    depth, d, n = mod_w.shape
    rows = cc.shape[0]
    tn = _tile(n, 1024)
    return pl.pallas_call(
        _mod_kernel,
        grid=(depth, n // tn),
        in_specs=[
            pl.BlockSpec((rows, d), lambda l, j: (0, 0)),
            pl.BlockSpec((1, d, tn), lambda l, j: (l, 0, j)),
            pl.BlockSpec((1, 1, tn), lambda l, j: (l, 0, j)),
        ],
        out_specs=pl.BlockSpec((1, rows, tn), lambda l, j: (l, 0, j)),
        out_shape=jax.ShapeDtypeStruct((depth, rows, n), F32),
        compiler_params=_cparams(("arbitrary", "arbitrary")),
        name="mod_vectors",
    )(cc, mod_w, mod_b.reshape(depth, 1, n))


def _ffn_kernel(h_ref, sh_ref, sc_ref, gt_ref, gpre_ref, gpost_ref, wg_ref, wu_ref, wd_ref,
                o_ref, u_ref, acc_ref, r_ref, *, nk):
    k = pl.program_id(1)

    @pl.when(k == 0)
    def _():
        _norm_modulate_rows(h_ref, u_ref, r_ref, gpre_ref[...] * (1.0 + sc_ref[0]), sh_ref[0])
        acc_ref[...] = jnp.zeros_like(acc_ref)

    u = u_ref[...]
    g = jnp.dot(u, wg_ref[...], preferred_element_type=F32)
    up = jnp.dot(u, wu_ref[...], preferred_element_type=F32)
    a = (_silu(g) * up).astype(BF16)
    acc_ref[...] += jnp.dot(a, wd_ref[...], preferred_element_type=F32)

    @pl.when(k == nk - 1)
    def _():
        _norm_residual_rows(acc_ref, h_ref, o_ref, r_ref, FFN_RES * gt_ref[0] * gpost_ref[...])


def _ffn(h, shift, scale, gate, g_pre, g_post, wg, wu, wd, widx, *, tm=512, tf=512):
    m, d = h.shape
    f = wg.shape[-1]
    wl, wh = widx
    bm = shift.shape[0]
    rows_per_b = m // bm
    tm = _tile(rows_per_b, tm)
    tf = _tile(f, tf)
    nk = f // tf
    per_b = rows_per_b // tm
    mod_spec = pl.BlockSpec((1, 1, d), lambda i, k: (i // per_b, 0, 0))
    vec_spec = pl.BlockSpec((1, d), lambda i, k: (0, 0))
    return pl.pallas_call(
        functools.partial(_ffn_kernel, nk=nk),
        grid=(m // tm, nk),
        in_specs=[
            pl.BlockSpec((tm, d), lambda i, k: (i, 0)),
            mod_spec, mod_spec, mod_spec, vec_spec, vec_spec,
            pl.BlockSpec((None, None, d, tf), lambda i, k: (wl, wh, 0, k)),
            pl.BlockSpec((None, None, d, tf), lambda i, k: (wl, wh, 0, k)),
            pl.BlockSpec((None, None, tf, d), lambda i, k: (wl, wh, k, 0)),
        ],
        out_specs=pl.BlockSpec((tm, d), lambda i, k: (i, 0)),
        out_shape=jax.ShapeDtypeStruct((m, d), F32),
        scratch_shapes=[pltpu.VMEM((tm, d), BF16), pltpu.VMEM((tm, d), F32), pltpu.VMEM((tm, LANES), F32)],
        compiler_params=_cparams(("parallel", "arbitrary")),
        name="ffn_half_step",
    )(h, shift, scale, gate, g_pre, g_post, wg, wu, wd)


def _proj_kernel(*refs, modulated):
    if modulated:
        x_ref, g_ref, sh_ref, sc_ref, w_ref, o_ref, u_ref, r_ref = refs
    else:
        x_ref, g_ref, w_ref, o_ref, u_ref, r_ref = refs

    @pl.when(pl.program_id(1) == 0)
    def _():
        if modulated:
            _norm_modulate_rows(x_ref, u_ref, r_ref, g_ref[...] * (1.0 + sc_ref[0]), sh_ref[0])
        else:
            _norm_modulate_rows(x_ref, u_ref, r_ref, g_ref[...], None)

    o_ref[...] = jnp.dot(u_ref[...], w_ref[...], preferred_element_type=F32).astype(o_ref.dtype)


def _proj(x, gain, w, *, out_dtype, shift=None, scale=None, kblock=0, tm=1024):
    m = x.shape[0]
    k, n = w.shape
    modulated = shift is not None
    bm = shift.shape[0] if modulated else 1
    rows_per_b = m // bm
    tm = _tile(rows_per_b, tm)
    per_b = rows_per_b // tm
    tn = n if k * n * 2 <= 6 * 1024 * 1024 else _tile(n, 1024)
    in_specs = [pl.BlockSpec((tm, k), lambda i, j: (i, kblock)),
                pl.BlockSpec((1, k), lambda i, j: (0, 0))]
    args = [x, gain]
    if modulated:
        mod_spec = pl.BlockSpec((1, 1, k), lambda i, j: (i // per_b, 0, 0))
        in_specs += [mod_spec, mod_spec]
        args += [shift, scale]
    in_specs.append(pl.BlockSpec((k, tn), lambda i, j: (0, j)))
    args.append(w)
    return pl.pallas_call(
        functools.partial(_proj_kernel, modulated=modulated),
        grid=(m // tm, n // tn),
        in_specs=in_specs,
        out_specs=pl.BlockSpec((tm, tn), lambda i, j: (i, j)),
        out_shape=jax.ShapeDtypeStruct((m, n), out_dtype),
        scratch_shapes=[pltpu.VMEM((tm, k), BF16), pltpu.VMEM((tm, LANES), F32)],
        compiler_params=_cparams(("parallel", "arbitrary")),
        name="norm_proj",
    )(*args)


def _out_proj_kernel(a_ref, w_ref, gpost_ref, gt_ref, h_ref, o_ref, acc_ref, r_ref, *, nk):
    if nk == 1:
        y = jnp.dot(a_ref[...], w_ref[...], preferred_element_type=F32)
        o_ref[...] = h_ref[...] + _rms(y) * (gt_ref[0] * gpost_ref[...])
        return
    k = pl.program_id(1)

    @pl.when(k == 0)
    def _():
        acc_ref[...] = jnp.zeros_like(acc_ref)

    acc_ref[...] += jnp.dot(a_ref[...], w_ref[...], preferred_element_type=F32)

    @pl.when(k == nk - 1)
    def _():
        _norm_residual_rows(acc_ref, h_ref, o_ref, r_ref, gt_ref[0] * gpost_ref[...])


def _out_proj(a, w, g_post, gate, h, *, tm=512, tk=2048):
    m, kdim = a.shape
    d = w.shape[1]
    bm = gate.shape[0]
    rows_per_b = m // bm
    tm = _tile(rows_per_b, tm)
    per_b = rows_per_b // tm
    tk = _tile(kdim, tk)
    nk = kdim // tk
    return pl.pallas_call(
        functools.partial(_out_proj_kernel, nk=nk),
        grid=(m // tm, nk),
        in_specs=[
            pl.BlockSpec((tm, tk), lambda i, k: (i, k)),
            pl.BlockSpec((tk, d), lambda i, k: (k, 0)),
            pl.BlockSpec((1, d), lambda i, k: (0, 0)),
            pl.BlockSpec((1, 1, d), lambda i, k: (i // per_b, 0, 0)),
            pl.BlockSpec((tm, d), lambda i, k: (i, 0)),
        ],
        out_specs=pl.BlockSpec((tm, d), lambda i, k: (i, 0)),
        out_shape=jax.ShapeDtypeStruct((m, d), F32),
        scratch_shapes=[pltpu.VMEM((tm, d), F32), pltpu.VMEM((tm, LANES), F32)],
        compiler_params=_cparams(("parallel", "arbitrary")),
        name="out_proj",
    )(a, w, g_post, gate, h)


def _rope128(x, c_ref, s_ref):
    return x * c_ref[...] + pltpu.roll(x, 64, axis=1) * s_ref[...]


def _prep_kernel(*refs, mode, rope, norm, scale, heads, xcol, x2col):
    refs = list(refs)
    o_ref = refs.pop()
    x_ref = refs.pop(0)
    x2_ref = refs.pop(0) if mode == "mla_k" else None
    g_ref = refs.pop(0) if norm else None
    c_ref, s_ref = (refs.pop(0), refs.pop(0)) if rope else (None, None)
    wx = HEAD_DIM if mode in ("head", "mla_k") else 2 * HEAD_DIM
    wo = HEAD_DIM if mode == "head" else 2 * HEAD_DIM

    if mode == "mla_k":
        shared = x2_ref[0, :, pl.ds(x2col * LANES, LANES)].astype(F32)
        if rope:
            shared = _rope128(shared, c_ref, s_ref)
        shared = shared.astype(o_ref.dtype)
    for h in range(heads):
        x = x_ref[0, :, pl.ds(xcol(h) * wx, wx)].astype(F32)
        if mode == "head":
            if norm:
                x = _rms(x) * g_ref[...]
            if rope:
                x = _rope128(x, c_ref, s_ref)
            o_ref[0, :, pl.ds(h * wo, wo)] = (x * scale).astype(o_ref.dtype)
        elif mode == "mla_q":
            r = x[:, HEAD_DIM:]
            if rope:
                r = _rope128(r, c_ref, s_ref)
            o_ref[0, :, pl.ds(h * wo, HEAD_DIM)] = (x[:, :HEAD_DIM] * scale).astype(o_ref.dtype)
            o_ref[0, :, pl.ds(h * wo + HEAD_DIM, HEAD_DIM)] = (r * scale).astype(o_ref.dtype)
        else:
            o_ref[0, :, pl.ds(h * wo, HEAD_DIM)] = x.astype(o_ref.dtype)
            o_ref[0, :, pl.ds(h * wo + HEAD_DIM, HEAD_DIM)] = shared


def _prep(x, *, mode, heads, xcol, x2=None, x2col=None, gain=None, tables=None, scale=1.0, ts=256):
    b, t, _ = x.shape
    ts = _tile(t, ts)
    wo = HEAD_DIM if mode == "head" else 2 * HEAD_DIM
    in_specs = [pl.BlockSpec((1, ts, x.shape[2]), lambda bb, i: (bb, i, 0))]
    args = [x]
    if mode == "mla_k":
        in_specs.append(pl.BlockSpec((1, ts, x2.shape[2]), lambda bb, i: (bb, i, 0)))
        args.append(x2)
    if gain is not None:
        in_specs.append(pl.BlockSpec((1, HEAD_DIM), lambda bb, i: (0, 0)))
        args.append(gain)
    if tables is not None:
        tspec = pl.BlockSpec((ts, LANES), lambda bb, i: (i, 0))
        in_specs += [tspec, tspec]
        args += list(tables)
    return pl.pallas_call(
        functools.partial(_prep_kernel, mode=mode, rope=tables is not None, norm=gain is not None,
                          scale=scale, heads=heads, xcol=xcol, x2col=x2col),
        grid=(b, t // ts),
        in_specs=in_specs,
        out_specs=pl.BlockSpec((1, ts, heads * wo), lambda bb, i: (bb, i, 0)),
        out_shape=jax.ShapeDtypeStruct((b, t, heads * wo), BF16),
        compiler_params=_cparams(("parallel", "parallel")),
        name="head_prep_" + mode,
    )(*args)


def _flash_kernel(*refs, nk, scale, has_lat, n_sub):
    if has_lat:
        q_ref, kc_ref, vc_ref, kl_ref, vl_ref, o_ref, m_ref, acc_ref, va_ref = refs
    else:
        q_ref, kc_ref, vc_ref, o_ref, m_ref, acc_ref, va_ref = refs
    kk = pl.program_id(3)
    tq = q_ref.shape[1]
    dv = o_ref.shape[2]
    tr = tq // n_sub

    def step(k, v):
        tk = k.shape[0]
        va_ref[pl.ds(0, tk), pl.ds(0, dv)] = v
        va = va_ref[pl.ds(0, tk), :]

        def scores(r):
            q = q_ref[0, pl.ds(r * tr, tr), :]
            if scale != 1.0:
                q = (q.astype(F32) * scale).astype(BF16)
            return lax.dot_general(q, k, (((1,), (1,)), ((), ())), preferred_element_type=F32)

        s_next = scores(0)
        for r in range(n_sub):
            rows = pl.ds(r * tr, tr)
            s = s_next
            if r + 1 < n_sub:
                s_next = scores(r + 1)
            m_prev = m_ref[rows, :]
            m_new = jnp.maximum(m_prev, jnp.max(s, axis=-1, keepdims=True))
            alpha = jnp.exp2(m_prev - m_new)
            p = jnp.exp2((s - _lane_tile(m_new, tk // LANES)).astype(BF16))
            acc_ref[rows, :] = (_lane_tile(alpha, 2 * dv // LANES) * acc_ref[rows, :]
                                + jnp.dot(p, va, preferred_element_type=F32))
            m_ref[rows, :] = m_new

    @pl.when(kk == 0)
    def _():
        m_ref[...] = jnp.full_like(m_ref, NEG_INF)
        acc_ref[...] = jnp.zeros_like(acc_ref)
        va_ref[:, pl.ds(dv, dv)] = jnp.ones((va_ref.shape[0], dv), va_ref.dtype)
        step(kc_ref[0], vc_ref[0])

    if has_lat:
        @pl.when(kk > 0)
        def _():
            step(kl_ref[0], vl_ref[0])

    @pl.when(kk == nk - 1)
    def _():
        o_ref[0] = (acc_ref[:, pl.ds(0, dv)] / acc_ref[:, pl.ds(dv, dv)]).astype(o_ref.dtype)


def _flash(q, kc, vc, kl=None, vl=None, *, heads, dq, dv, qcol, kcol, vcol, scale=1.0,
           tq=8192, tk=2048, tr=512):
    b, s, _ = q.shape
    tc = kc.shape[1]
    tq = _tile(s, tq)
    has_lat = kl is not None
    if has_lat:
        tk = _tile(kl.shape[1], tk)
        nk = 1 + kl.shape[1] // tk
    else:
        nk = 1
    in_specs = [
        pl.BlockSpec((1, tq, dq), lambda bb, h, i, kk: (bb, i, qcol(h))),
        pl.BlockSpec((1, tc, dq), lambda bb, h, i, kk: (bb, 0, kcol(h))),
        pl.BlockSpec((1, tc, dv), lambda bb, h, i, kk: (bb, 0, vcol(h))),
    ]
    args = [q, kc, vc]
    if has_lat:
        in_specs += [
            pl.BlockSpec((1, tk, dq), lambda bb, h, i, kk: (bb, jnp.maximum(kk - 1, 0), kcol(h))),
            pl.BlockSpec((1, tk, dv), lambda bb, h, i, kk: (bb, jnp.maximum(kk - 1, 0), vcol(h))),
        ]
        args += [kl, vl]
    return pl.pallas_call(
        functools.partial(_flash_kernel, nk=nk, scale=scale, has_lat=has_lat, n_sub=tq // _tile(tq, tr)),
        grid=(b, heads, s // tq, nk),
        in_specs=in_specs,
        out_specs=pl.BlockSpec((1, tq, dv), lambda bb, h, i, kk: (bb, i, h)),
        out_shape=jax.ShapeDtypeStruct((b, s, heads * dv), BF16),
        scratch_shapes=[pltpu.VMEM((tq, LANES), F32), pltpu.VMEM((tq, 2 * dv), F32),
                        pltpu.VMEM((max(tc, tk), 2 * dv), BF16)],
        compiler_params=_cparams(("parallel", "parallel", "parallel", "arbitrary")),
        name="flash_attention",
    )(*args)


def _na_kernel(q_ref, k_ref, v_ref, kc_ref, vc_ref, *rest, band, seq, scale, n_sub, blk):
    bias_refs, o_ref = rest[:-1], rest[-1]
    i = pl.program_id(2)
    nb = len(bias_refs)
    dv = o_ref.shape[2]
    kc = kc_ref[0]
    vc = jnp.concatenate([vc_ref[0], jnp.ones((kc.shape[0], dv), BF16)], axis=1)
    dn = (((1,), (1,)), ((), ()))
    tr = blk // n_sub
    for j in range(nb):
        start = jnp.clip((i * nb + j) * blk - (band - blk) // 2, 0, seq - band)
        start = pl.multiple_of(start, 256)
        kb = k_ref[0, pl.ds(start, band), :]
        vb = jnp.concatenate([v_ref[0, pl.ds(start, band), :], jnp.ones((band, dv), BF16)], axis=1)
        for r in range(n_sub):
            rows = pl.ds(j * blk + r * tr, tr)
            q = (q_ref[0, rows, :].astype(F32) * scale).astype(BF16)
            s_nb = (lax.dot_general(q, kb, dn, preferred_element_type=F32)
                    + bias_refs[j][0, 0, pl.ds(r * tr, tr), :])
            s_cx = lax.dot_general(q, kc, dn, preferred_element_type=F32)
            m = jnp.maximum(jnp.max(s_nb, axis=-1, keepdims=True), jnp.max(s_cx, axis=-1, keepdims=True))
            p_nb = jnp.exp2((s_nb - m).astype(BF16))
            p_cx = jnp.exp2((s_cx - m).astype(BF16))
            o = (jnp.dot(p_nb, vb, preferred_element_type=F32)
                 + jnp.dot(p_cx, vc, preferred_element_type=F32))
            o_ref[0, rows, :] = (o[:, :dv] / o[:, dv:]).astype(o_ref.dtype)


def _na_bias_tiles(rpb, rows):
    nr, wr, wc, w = NA_QROWS, NA_WIN_R, NA_WIN_C, GRID_W
    brows = nr + wr
    heads = rpb.shape[0]
    lpad = w - wc
    vp = jnp.pad(rpb.astype(F32) * LOG2_E, ((0, 0), (0, 0), (lpad, lpad)), constant_values=NEG_INF)
    toep = jnp.stack([vp[:, :, w - 1 - c: 2 * w - 1 - c] for c in range(w)], axis=2)
    col = jnp.arange(w, dtype=jnp.int32)
    c0 = jnp.clip(col - wc // 2, 0, w - wc)
    in_col = (col[None, :] >= c0[:, None]) & (col[None, :] < c0[:, None] + wc)
    toep = jnp.where(in_col[None, None], toep, NEG_INF)
    masked = jnp.full((heads, 1, w, w), NEG_INF, F32)
    tiles = []
    for r_first, b_first in ((0, 0), (nr, nr - wr // 2), (rows - nr, rows - brows)):
        per_row = []
        for a in range(nr):
            r = r_first + a
            r0 = min(max(r - wr // 2, 0), rows - wr)
            lo = r0 - b_first
            dr_lo = r0 - r + (wr - 1)
            blk = toep[:, dr_lo: dr_lo + wr]
            blk = jnp.concatenate([jnp.tile(masked, (1, lo, 1, 1)), blk,
                                   jnp.tile(masked, (1, brows - wr - lo, 1, 1))], axis=1)
            per_row.append(jnp.swapaxes(blk, 1, 2))
        tiles.append(jnp.stack(per_row, axis=1).reshape(heads, nr * w, brows * w))
    return jnp.stack(tiles, axis=1)


def _na_attention(z, zc, rpb, *, heads):
    b, s, _ = z.shape
    tcx = zc.shape[1]
    rows = s // GRID_W
    tq = NA_QROWS * GRID_W
    band = (NA_QROWS + NA_WIN_R) * GRID_W
    nblk = s // tq
    bias = _na_bias_tiles(rpb, rows)
    scale = HEAD_DIM ** -0.5 * LOG2_E

    def variant(i):
        return jnp.where(i == 0, 0, jnp.where(i == nblk - 1, 2, 1))

    nb = 2 if nblk % 2 == 0 else 1
    bias_specs = [pl.BlockSpec((1, 1, tq, band), lambda bb, h, i, j=j: (h, variant(i * nb + j), 0, 0))
                  for j in range(nb)]
    return pl.pallas_call(
        functools.partial(_na_kernel, band=band, seq=s, scale=scale, n_sub=4, blk=tq),
        grid=(b, heads, nblk // nb),
        in_specs=[
            pl.BlockSpec((1, nb * tq, HEAD_DIM), lambda bb, h, i: (bb, i, h)),
            pl.BlockSpec((1, s, HEAD_DIM), lambda bb, h, i: (bb, 0, heads + h)),
            pl.BlockSpec((1, s, HEAD_DIM), lambda bb, h, i: (bb, 0, 2 * heads + h)),
            pl.BlockSpec((1, tcx, HEAD_DIM), lambda bb, h, i: (bb, 0, heads + h)),
            pl.BlockSpec((1, tcx, HEAD_DIM), lambda bb, h, i: (bb, 0, 2 * heads + h)),
        ] + bias_specs,
        out_specs=pl.BlockSpec((1, nb * tq, HEAD_DIM), lambda bb, h, i: (bb, i, h)),
        out_shape=jax.ShapeDtypeStruct((b, s, heads * HEAD_DIM), BF16),
        compiler_params=_cparams(("parallel", "parallel", "arbitrary")),
        name="neighbourhood_attention",
    )(z, z, z, zc, zc, *([bias] * nb))


def _ret_kernel(dec_ref, *refs, nc):
    ins, (of_ref, ob_ref, st_ref, din_ref) = refs[:14], refs[14:]
    dirs = (ins[:7] + (of_ref,), ins[7:] + (ob_ref,))
    h = pl.program_id(1)
    t = pl.program_id(2)
    half = RET_QK // 2
    lgs = [jnp.log1p(-jnp.exp2(jnp.full((1, 1), dec_ref[d, h], F32))) for d in (0, 1)]

    def rope(x, cos_ref, sin_ref):
        c, s = cos_ref[...], sin_ref[...]
        x1, x2 = x[:, :half], x[:, half:]
        return jnp.concatenate([x1 * c - x2 * s, x2 * c + x1 * s], axis=-1)

    def update(d, k, v):
        n = k.shape[0]
        pos = lax.broadcasted_iota(jnp.int32, (n, 1), 0).astype(F32)
        zeta = jnp.exp((n - 1.0 - pos if d == 0 else pos) * lgs[d])
        kz = (k * zeta).astype(BF16)
        kv = lax.dot_general(kz, v, (((0,), (0,)), ((), ())), preferred_element_type=F32)
        st_ref[d] = st_ref[d] * jnp.exp(n * lgs[d]) + kv

    @pl.when(t == 0)
    def _():
        st_ref[...] = jnp.zeros_like(st_ref)
        n = din_ref.shape[1]
        ri = lax.broadcasted_iota(jnp.int32, (n, n), 0)
        ci = lax.broadcasted_iota(jnp.int32, (n, n), 1)
        for d in (0, 1):
            dist = ri - ci if d == 0 else ci - ri
            din_ref[d] = jnp.where(dist >= 0, jnp.exp(jnp.maximum(dist, 0).astype(F32) * lgs[d]), 0.0)

    @pl.when(t < nc)
    def _():
        for d in (0, 1):
            kc_ref, vc_ref = dirs[d][3], dirs[d][4]
            update(d, kc_ref[0].astype(F32) * RET_QK ** -0.5, vc_ref[0])

    @pl.when(t >= nc)
    def _():
        for d in (0, 1):
            q_ref, k_ref, v_ref, _, _, cos_ref, sin_ref, o_ref = dirs[d]
            q = rope(q_ref[0].astype(F32), cos_ref, sin_ref)
            k = rope(k_ref[0].astype(F32) * RET_QK ** -0.5, cos_ref, sin_ref)
            v = v_ref[0]
            n = q.shape[0]
            pos = lax.broadcasted_iota(jnp.int32, (n, 1), 0).astype(F32)
            qb = q.astype(BF16)
            s = lax.dot_general(qb, k.astype(BF16), (((1,), (1,)), ((), ())),
                                preferred_element_type=F32) * din_ref[d]
            xi = jnp.exp((pos + 1.0 if d == 0 else n - pos) * lgs[d])
            o = (jnp.dot(s.astype(BF16), v, preferred_element_type=F32)
                 + jnp.dot(qb, st_ref[d].astype(BF16), preferred_element_type=F32) * xi)
            o_ref[0] = o.astype(o_ref.dtype)
            update(d, k, v)


def _retention(z, zc, decay_log2, cos, sin, *, heads):
    b, s, _ = z.shape
    tcx = zc.shape[1]
    chunk = _tile(s, RET_CHUNK)
    cchunk = _tile(tcx, RET_CHUNK)
    nc, nl = tcx // cchunk, s // chunk
    kcol0 = heads
    vcol0 = (2 * heads * RET_QK) // RET_V

    def cidx(d, t):
        c = jnp.minimum(t, nc - 1)
        return c if d == 0 else nc - 1 - c

    def lidx(d, t):
        c = jnp.maximum(t - nc, 0)
        return c if d == 0 else nl - 1 - c

    in_specs = [pl.BlockSpec(memory_space=pltpu.SMEM)]
    out_specs = []
    for d in (0, 1):
        in_specs += [
            pl.BlockSpec((1, chunk, RET_QK), lambda bb, h, t, d=d: (bb, lidx(d, t), h)),
            pl.BlockSpec((1, chunk, RET_QK), lambda bb, h, t, d=d: (bb, lidx(d, t), kcol0 + h)),
            pl.BlockSpec((1, chunk, RET_V), lambda bb, h, t, d=d: (bb, lidx(d, t), vcol0 + h)),
            pl.BlockSpec((1, cchunk, RET_QK), lambda bb, h, t, d=d: (bb, cidx(d, t), kcol0 + h)),
            pl.BlockSpec((1, cchunk, RET_V), lambda bb, h, t, d=d: (bb, cidx(d, t), vcol0 + h)),
            pl.BlockSpec((chunk, RET_QK // 2), lambda bb, h, t, d=d: (lidx(d, t), 0)),
            pl.BlockSpec((chunk, RET_QK // 2), lambda bb, h, t, d=d: (lidx(d, t), 0)),
        ]
        out_specs.append(pl.BlockSpec((1, chunk, RET_V), lambda bb, h, t, d=d: (bb, lidx(d, t), h)))
    out = jax.ShapeDtypeStruct((b, s, heads * RET_V), BF16)
    return pl.pallas_call(
        functools.partial(_ret_kernel, nc=nc),
        grid=(b, heads, nc + nl),
        in_specs=in_specs,
        out_specs=out_specs,
        out_shape=[out, out],
        scratch_shapes=[pltpu.VMEM((2, RET_QK, RET_V), F32), pltpu.VMEM((2, chunk, chunk), F32)],
        compiler_params=_cparams(("parallel", "parallel", "arbitrary")),
        name="retention_scan",
    )(decay_log2, *([z, z, z, zc, zc, cos, sin] * 2))


def _ret_gate_kernel(of_ref, ob_ref, g_ref, y_ref, *, heads):
    for h in range(heads):
        cols = pl.ds(h * RET_V, RET_V)
        o = of_ref[0, :, cols].astype(F32) + ob_ref[0, :, cols].astype(F32)
        y_ref[0, :, cols] = (_silu(g_ref[0, :, cols].astype(F32)) * _rms(o)).astype(y_ref.dtype)


def _ret_gate(o_fwd, o_bwd, z, *, heads, ts=256):
    b, s, width = o_fwd.shape
    ts = _tile(s, ts)
    gcol = (2 * heads * RET_QK + heads * RET_V) // width
    spec = pl.BlockSpec((1, ts, width), lambda bb, i: (bb, i, 0))
    return pl.pallas_call(
        functools.partial(_ret_gate_kernel, heads=heads),
        grid=(b, s // ts),
        in_specs=[spec, spec, pl.BlockSpec((1, ts, width), lambda bb, i: (bb, i, gcol))],
        out_specs=spec,
        out_shape=jax.ShapeDtypeStruct((b, s, width), BF16),
        compiler_params=_cparams(("parallel", "parallel")),
        name="retention_gate",
    )(o_fwd, o_bwd, z)


def _axial_angles(n_tokens, rot_dim):
    t = jnp.arange(n_tokens, dtype=jnp.int32)
    row = (t // GRID_W).astype(F32)
    col = (t % GRID_W).astype(F32)
    n_f = rot_dim // 4
    freqs = ROPE_BASE ** (-jnp.arange(n_f, dtype=F32) / n_f)
    return jnp.concatenate([row[:, None] * freqs, col[:, None] * freqs], axis=-1)


def _rope_slot_tables(ang):
    pad = jnp.zeros((ang.shape[0], 64 - ang.shape[1]), F32)
    cos, sin = jnp.cos(ang), jnp.sin(ang)
    return (jnp.concatenate([cos, pad, cos, pad], axis=-1),
            jnp.concatenate([-sin, pad, sin, pad], axis=-1))


def _spread_rope_cols(w):
    half = MLA_ROPE // 2
    z = jnp.zeros(w.shape[:-1] + (64 - half,), w.dtype)
    return jnp.concatenate([w[..., :half], z, w[..., half:], z], axis=-1)


def _mla_mixer(in_proj, b, s, tcx, heads, w_in, q_gain, kv_gain, w_uq, w_ukv):
    rq = q_gain.shape[-1]
    w_in = jnp.concatenate([w_in[:, :2 * rq], _spread_rope_cols(w_in[:, 2 * rq:])], axis=-1).astype(BF16)
    w_uq = w_uq.reshape(rq, heads, HEAD_DIM + MLA_ROPE)
    w_uq = jnp.concatenate([w_uq[..., :HEAD_DIM], _spread_rope_cols(w_uq[..., HEAD_DIM:])],
                           axis=-1).reshape(rq, heads * 2 * HEAD_DIM).astype(BF16)
    w_ukv = w_ukv.astype(BF16)
    tables = _rope_slot_tables(_axial_angles(s, MLA_ROPE))
    scale = (HEAD_DIM + MLA_ROPE) ** -0.5 * LOG2_E
    zl, zc = in_proj(w_in, F32)
    ropecol = 2 * rq // LANES

    def qkv(z, n_tok, tabs):
        q = _proj(z, q_gain[None], w_uq, kblock=0, out_dtype=BF16).reshape(b, n_tok, -1)
        kv = _proj(z, kv_gain[None], w_ukv, kblock=1, out_dtype=BF16).reshape(b, n_tok, -1)
        q = _prep(q, mode="mla_q", heads=heads, xcol=lambda hh: hh, tables=tabs, scale=scale)
        k = _prep(kv, mode="mla_k", heads=heads, xcol=lambda hh: 2 * hh,
                  x2=z.reshape(b, n_tok, -1), x2col=ropecol, tables=tabs)
        return q, k, kv

    ql, kl, kvl = qkv(zl, s, tables)
    qc, kc, kvc = qkv(zc, tcx, None)
    cols = dict(heads=heads, dq=2 * HEAD_DIM, dv=HEAD_DIM, qcol=lambda hh: hh, kcol=lambda hh: hh,
                vcol=lambda hh: 2 * hh + 1)
    return _flash(ql, kc, kvc, kl, kvl, **cols), _flash(qc, kc, kvc, **cols)


def _gqa_mixer(in_proj, b, s, tcx, heads, w_in, q_gain, k_gain):
    hk = (w_in.shape[-1] // HEAD_DIM - heads) // 2
    grp = heads // hk
    tables = _rope_slot_tables(_axial_angles(s, HEAD_DIM))
    zl, zc = in_proj(w_in.astype(BF16), BF16)
    zl = zl.reshape(b, s, -1)
    zc = zc.reshape(b, tcx, -1)
    scale = HEAD_DIM ** -0.5 * LOG2_E
    qg, kg = q_gain[None], k_gain[None]
    ql = _prep(zl, mode="head", heads=heads, xcol=lambda hh: hh, gain=qg, tables=tables, scale=scale)
    kl = _prep(zl, mode="head", heads=hk, xcol=lambda hh: heads + hh, gain=kg, tables=tables)
    qc = _prep(zc, mode="head", heads=heads, xcol=lambda hh: hh, gain=qg, scale=scale)
    kc = _prep(zc, mode="head", heads=hk, xcol=lambda hh: heads + hh, gain=kg)
    cols = dict(heads=heads, dq=HEAD_DIM, dv=HEAD_DIM, qcol=lambda hh: hh,
                kcol=lambda hh: hh // grp, vcol=lambda hh: heads + hk + hh // grp)
    return _flash(ql, kc, zc, kl, zl, **cols), _flash(qc, kc, zc, **cols)


def _na_mixer(in_proj, b, s, tcx, heads, w_in, rpb):
    zl, zc = in_proj(w_in.astype(BF16), BF16)
    zl = zl.reshape(b, s, -1)
    zc = zc.reshape(b, tcx, -1)
    y = _na_attention(zl, zc, rpb, heads=heads)
    yc = _flash(zc, zc, zc, heads=heads, dq=HEAD_DIM, dv=HEAD_DIM, qcol=lambda hh: hh,
                kcol=lambda hh: heads + hh, vcol=lambda hh: 2 * heads + hh,
                scale=HEAD_DIM ** -0.5 * LOG2_E)
    return y, yc


def _ret_mixer(in_proj, b, s, tcx, w_in, decay_log2):
    rh = decay_log2.shape[-1]
    pos = jnp.arange(s, dtype=F32)
    freqs = ROPE_BASE ** (-jnp.arange(RET_QK // 2, dtype=F32) / (RET_QK // 2))
    ang = pos[:, None] * freqs
    zl, zc = in_proj(w_in.astype(BF16), BF16)
    zl = zl.reshape(b, s, -1)
    zc = zc.reshape(b, tcx, -1)
    o_fwd, o_bwd = _retention(zl, zc, decay_log2.astype(F32), jnp.cos(ang), jnp.sin(ang), heads=rh)
    return _ret_gate(o_fwd, o_bwd, zl, heads=rh)


def kernel(x, c, ctx, c_ctx, mod_w, mod_b, norm_g, ffn_w_gate, ffn_w_up, ffn_w_down,
           mla_w_in, mla_q_gain, mla_kv_gain, mla_w_uq, mla_w_ukv, mla_w_o,
           gqa_w_in, gqa_q_gain, gqa_k_gain, gqa_w_o,
           na_w_in, na_rpb, na_w_o,
           ret_w_in, ret_decay_log2, ret_w_o):
    b, s, d = x.shape
    tcx = ctx.shape[1]
    depth = mod_w.shape[0]
    assert depth == 4 and s % (NA_QROWS * GRID_W) == 0 and s // GRID_W >= NA_QROWS + NA_WIN_R
    heads = d // HEAD_DIM

    cc = jnp.concatenate([c, c_ctx[None], jnp.zeros((8 - b - 1, d), F32)], axis=0)
    mods = _mod_vectors(cc, mod_w, mod_b).reshape(depth, 8, N_MOD, d)

    h = x.reshape(b * s, d)
    hc = ctx.reshape(b * tcx, d)
    w_gate, w_up, w_down = ffn_w_gate.astype(BF16), ffn_w_up.astype(BF16), ffn_w_down.astype(BF16)

    for i in range(depth):
        kind = i % 4
        last = i == depth - 1
        ml = [mods[i, :b, j][:, None, :] for j in range(N_MOD)]
        mc = [mods[i, b:b + 1, j][:, None, :] for j in range(N_MOD)]
        gn = [norm_g[i, j][None, :] for j in range(6)]

        def ffn(hh, m, f):
            return _ffn(hh, m[3 * f], m[3 * f + 1], m[3 * f + 2], gn[2 * f], gn[2 * f + 1],
                        w_gate, w_up, w_down, (i, f // 2))

        h = ffn(h, ml, 0)
        hc = ffn(hc, mc, 0)

        def in_proj(w, out_dtype):
            zl = _proj(h, gn[2], w, shift=ml[3], scale=ml[4], out_dtype=out_dtype)
            zc = _proj(hc, gn[2], w, shift=mc[3], scale=mc[4], out_dtype=out_dtype)
            return zl, zc

        if kind == 0:
            y, yc = _mla_mixer(in_proj, b, s, tcx, heads, mla_w_in[0], mla_q_gain[0], mla_kv_gain[0],
                               mla_w_uq[0], mla_w_ukv[0])
            w_o = mla_w_o[0]
        elif kind == 1:
            y, yc = _gqa_mixer(in_proj, b, s, tcx, heads, gqa_w_in[0], gqa_q_gain[0], gqa_k_gain[0])
            w_o = gqa_w_o[0]
        elif kind == 2:
            y, yc = _na_mixer(in_proj, b, s, tcx, heads, na_w_in[0], na_rpb[0])
            w_o = na_w_o[0]
        else:
            y, yc = _ret_mixer(in_proj, b, s, tcx, ret_w_in[0], ret_decay_log2[0]), None
            w_o = ret_w_o[0]

        w_o = w_o.astype(BF16)
        h = _out_proj(y.reshape(b * s, -1), w_o, gn[3], ml[5], h)
        h = ffn(h, ml, 2)
        if not last:
            hc = _out_proj(yc.reshape(b * tcx, -1), w_o, gn[3], mc[5], hc)
            hc = ffn(hc, mc, 2)
    return h.reshape(b, s, d)
```

```python
import functools

import jax
import jax.numpy as jnp
from jax import lax
from jax.experimental import pallas as pl
from jax.experimental.pallas import tpu as pltpu

F32 = jnp.float32
BF16 = jnp.bfloat16

RMS_EPS = 1e-6
ROPE_BASE = 10000.0
FFN_RES = 0.5
GRID_W = 64
N_MOD = 9
LANES = 128
ROW_CHUNK = 64
APPLY_CHUNK = 32
HEAD_DIM = 128
MLA_ROPE = 64
RET_QK = 256
RET_V = 512
RET_CHUNK = 512
NA_WIN_R = 8
NA_WIN_C = 16
NA_QROWS = 8
NEG_INF = -1e30
LOG2_E = 1.4426950408889634
VMEM_LIMIT_BYTES = 56 * 1024 * 1024


def _cparams(semantics):
    return pltpu.CompilerParams(dimension_semantics=semantics, vmem_limit_bytes=VMEM_LIMIT_BYTES)


def _tile(n, pref):
    if n <= pref:
        return n
    t = pref
    while n % t:
        t //= 2
    return t


def _rms(x):
    return x * lax.rsqrt(jnp.mean(x * x, axis=-1, keepdims=True) + RMS_EPS)


def _silu(x):
    return x * jax.nn.sigmoid(x)


def _row_chunks(n_rows):
    step = min(ROW_CHUNK, n_rows)
    return [pl.ds(r, step) for r in range(0, n_rows, step)]


def _row_rsqrt(x_ref, r_ref):
    for rows in _row_chunks(x_ref.shape[0]):
        x = x_ref[rows, :].astype(F32)
        ms = jnp.mean(x * x, axis=-1, keepdims=True)
        r_ref[rows, :] = jnp.broadcast_to(lax.rsqrt(ms + RMS_EPS), (rows.size, LANES))


def _loop_row_chunks(n_rows, body):
    step = min(APPLY_CHUNK, n_rows)

    def it(c, carry):
        body(pl.ds(pl.multiple_of(c * step, step), step))
        return carry

    lax.fori_loop(0, n_rows // step, it, 0, unroll=2)


def _norm_modulate_rows(x_ref, u_ref, r_ref, gain, shift):
    _row_rsqrt(x_ref, r_ref)

    def apply(rows):
        u = x_ref[rows, :].astype(F32) * _lane_tile(r_ref[rows, :], x_ref.shape[1] // LANES) * gain
        u_ref[rows, :] = (u if shift is None else u + shift).astype(u_ref.dtype)

    _loop_row_chunks(x_ref.shape[0], apply)


def _norm_residual_rows(y_ref, h_ref, o_ref, r_ref, gain):
    _row_rsqrt(y_ref, r_ref)

    def apply(rows):
        y = y_ref[rows, :] * _lane_tile(r_ref[rows, :], y_ref.shape[1] // LANES)
        o_ref[rows, :] = h_ref[rows, :] + y * gain

    _loop_row_chunks(y_ref.shape[0], apply)


def _lane_tile(x, n):
    return x if n == 1 else jnp.concatenate([x] * n, axis=1)


def _mod_kernel(cc_ref, w_ref, b_ref, o_ref):
    s = _silu(cc_ref[...]).astype(BF16)
    o_ref[0] = jnp.dot(s, w_ref[0].astype(BF16), preferred_element_type=F32) + b_ref[0]


def _mod_vectors(cc, mod_w, mod_b):
    depth, d, n = mod_w.shape
    rows = cc.shape[0]
    tn = _tile(n, 1024)
    return pl.pallas_call(
        _mod_kernel,
        grid=(depth, n // tn),
        in_specs=[
            pl.BlockSpec((rows, d), lambda l, j: (0, 0)),
            pl.BlockSpec((1, d, tn), lambda l, j: (l, 0, j)),
            pl.BlockSpec((1, 1, tn), lambda l, j: (l, 0, j)),
        ],
        out_specs=pl.BlockSpec((1, rows, tn), lambda l, j: (l, 0, j)),
        out_shape=jax.ShapeDtypeStruct((depth, rows, n), F32),
        compiler_params=_cparams(("arbitrary", "arbitrary")),
        name="mod_vectors",
    )(cc, mod_w, mod_b.reshape(depth, 1, n))


def _ffn_kernel(h_ref, sh_ref, sc_ref, gt_ref, gpre_ref, gpost_ref, wg_ref, wu_ref, wd_ref,
                o_ref, u_ref, acc_ref, r_ref, *, nk):
    k = pl.program_id(1)

    @pl.when(k == 0)
    def _():
        _norm_modulate_rows(h_ref, u_ref, r_ref, gpre_ref[...] * (1.0 + sc_ref[0]), sh_ref[0])
        acc_ref[...] = jnp.zeros_like(acc_ref)

    u = u_ref[...]
    g = jnp.dot(u, wg_ref[...], preferred_element_type=F32)
    up = jnp.dot(u, wu_ref[...], preferred_element_type=F32)
    a = (_silu(g) * up).astype(BF16)
    acc_ref[...] += jnp.dot(a, wd_ref[...], preferred_element_type=F32)

    @pl.when(k == nk - 1)
    def _():
        _norm_residual_rows(acc_ref, h_ref, o_ref, r_ref, FFN_RES * gt_ref[0] * gpost_ref[...])


def _ffn(h, shift, scale, gate, g_pre, g_post, wg, wu, wd, widx, *, tm=512, tf=512):
    m, d = h.shape
    f = wg.shape[-1]
    wl, wh = widx
    bm = shift.shape[0]
    rows_per_b = m // bm
    tm = _tile(rows_per_b, tm)
    tf = _tile(f, tf)
    nk = f // tf
    per_b = rows_per_b // tm
    mod_spec = pl.BlockSpec((1, 1, d), lambda i, k: (i // per_b, 0, 0))
    vec_spec = pl.BlockSpec((1, d), lambda i, k: (0, 0))
    return pl.pallas_call(
        functools.partial(_ffn_kernel, nk=nk),
        grid=(m // tm, nk),
        in_specs=[
            pl.BlockSpec((tm, d), lambda i, k: (i, 0)),
            mod_spec, mod_spec, mod_spec, vec_spec, vec_spec,
            pl.BlockSpec((None, None, d, tf), lambda i, k: (wl, wh, 0, k)),
            pl.BlockSpec((None, None, d, tf), lambda i, k: (wl, wh, 0, k)),
            pl.BlockSpec((None, None, tf, d), lambda i, k: (wl, wh, k, 0)),
        ],
        out_specs=pl.BlockSpec((tm, d), lambda i, k: (i, 0)),
        out_shape=jax.ShapeDtypeStruct((m, d), F32),
        scratch_shapes=[pltpu.VMEM((tm, d), BF16), pltpu.VMEM((tm, d), F32), pltpu.VMEM((tm, LANES), F32)],
        compiler_params=_cparams(("parallel", "arbitrary")),
        name="ffn_half_step",
    )(h, shift, scale, gate, g_pre, g_post, wg, wu, wd)


def _proj_kernel(*refs, modulated):
    if modulated:
        x_ref, g_ref, sh_ref, sc_ref, w_ref, o_ref, u_ref, r_ref = refs
    else:
        x_ref, g_ref, w_ref, o_ref, u_ref, r_ref = refs

    @pl.when(pl.program_id(1) == 0)
    def _():
        if modulated:
            _norm_modulate_rows(x_ref, u_ref, r_ref, g_ref[...] * (1.0 + sc_ref[0]), sh_ref[0])
        else:
            _norm_modulate_rows(x_ref, u_ref, r_ref, g_ref[...], None)

    o_ref[...] = jnp.dot(u_ref[...], w_ref[...], preferred_element_type=F32).astype(o_ref.dtype)


def _proj(x, gain, w, *, out_dtype, shift=None, scale=None, kblock=0, tm=1024):
    m = x.shape[0]
    k, n = w.shape
    modulated = shift is not None
    bm = shift.shape[0] if modulated else 1
    rows_per_b = m // bm
    tm = _tile(rows_per_b, tm)
    per_b = rows_per_b // tm
    tn = n if k * n * 2 <= 6 * 1024 * 1024 else _tile(n, 1024)
    in_specs = [pl.BlockSpec((tm, k), lambda i, j: (i, kblock)),
                pl.BlockSpec((1, k), lambda i, j: (0, 0))]
    args = [x, gain]
    if modulated:
        mod_spec = pl.BlockSpec((1, 1, k), lambda i, j: (i // per_b, 0, 0))
        in_specs += [mod_spec, mod_spec]
        args += [shift, scale]
    in_specs.append(pl.BlockSpec((k, tn), lambda i, j: (0, j)))
    args.append(w)
    return pl.pallas_call(
        functools.partial(_proj_kernel, modulated=modulated),
        grid=(m // tm, n // tn),
        in_specs=in_specs,
        out_specs=pl.BlockSpec((tm, tn), lambda i, j: (i, j)),
        out_shape=jax.ShapeDtypeStruct((m, n), out_dtype),
        scratch_shapes=[pltpu.VMEM((tm, k), BF16), pltpu.VMEM((tm, LANES), F32)],
        compiler_params=_cparams(("parallel", "arbitrary")),
        name="norm_proj",
    )(*args)


def _out_proj_kernel(a_ref, w_ref, gpost_ref, gt_ref, h_ref, o_ref, acc_ref, r_ref, *, nk):
    if nk == 1:
        y = jnp.dot(a_ref[...], w_ref[...], preferred_element_type=F32)
        o_ref[...] = h_ref[...] + _rms(y) * (gt_ref[0] * gpost_ref[...])
        return
    k = pl.program_id(1)

    @pl.when(k == 0)
    def _():
        acc_ref[...] = jnp.zeros_like(acc_ref)

    acc_ref[...] += jnp.dot(a_ref[...], w_ref[...], preferred_element_type=F32)

    @pl.when(k == nk - 1)
    def _():
        _norm_residual_rows(acc_ref, h_ref, o_ref, r_ref, gt_ref[0] * gpost_ref[...])


def _out_proj(a, w, g_post, gate, h, *, tm=512, tk=2048):
    m, kdim = a.shape
    d = w.shape[1]
    bm = gate.shape[0]
    rows_per_b = m // bm
    tm = _tile(rows_per_b, tm)
    per_b = rows_per_b // tm
    tk = _tile(kdim, tk)
    nk = kdim // tk
    return pl.pallas_call(
        functools.partial(_out_proj_kernel, nk=nk),
        grid=(m // tm, nk),
        in_specs=[
            pl.BlockSpec((tm, tk), lambda i, k: (i, k)),
            pl.BlockSpec((tk, d), lambda i, k: (k, 0)),
            pl.BlockSpec((1, d), lambda i, k: (0, 0)),
            pl.BlockSpec((1, 1, d), lambda i, k: (i // per_b, 0, 0)),
            pl.BlockSpec((tm, d), lambda i, k: (i, 0)),
        ],
        out_specs=pl.BlockSpec((tm, d), lambda i, k: (i, 0)),
        out_shape=jax.ShapeDtypeStruct((m, d), F32),
        scratch_shapes=[pltpu.VMEM((tm, d), F32), pltpu.VMEM((tm, LANES), F32)],
        compiler_params=_cparams(("parallel", "arbitrary")),
        name="out_proj",
    )(a, w, g_post, gate, h)


def _rope128(x, c_ref, s_ref):
    return x * c_ref[...] + pltpu.roll(x, 64, axis=1) * s_ref[...]


def _prep_kernel(*refs, mode, rope, norm, scale, heads, xcol, x2col):
    refs = list(refs)
    o_ref = refs.pop()
    x_ref = refs.pop(0)
    x2_ref = refs.pop(0) if mode == "mla_k" else None
    g_ref = refs.pop(0) if norm else None
    c_ref, s_ref = (refs.pop(0), refs.pop(0)) if rope else (None, None)
    wx = HEAD_DIM if mode in ("head", "mla_k") else 2 * HEAD_DIM
    wo = HEAD_DIM if mode == "head" else 2 * HEAD_DIM

    if mode == "mla_k":
        shared = x2_ref[0, :, pl.ds(x2col * LANES, LANES)].astype(F32)
        if rope:
            shared = _rope128(shared, c_ref, s_ref)
        shared = shared.astype(o_ref.dtype)
    for h in range(heads):
        x = x_ref[0, :, pl.ds(xcol(h) * wx, wx)].astype(F32)
        if mode == "head":
            if norm:
                x = _rms(x) * g_ref[...]
            if rope:
                x = _rope128(x, c_ref, s_ref)
            o_ref[0, :, pl.ds(h * wo, wo)] = (x * scale).astype(o_ref.dtype)
        elif mode == "mla_q":
            r = x[:, HEAD_DIM:]
            if rope:
                r = _rope128(r, c_ref, s_ref)
            o_ref[0, :, pl.ds(h * wo, HEAD_DIM)] = (x[:, :HEAD_DIM] * scale).astype(o_ref.dtype)
            o_ref[0, :, pl.ds(h * wo + HEAD_DIM, HEAD_DIM)] = (r * scale).astype(o_ref.dtype)
        else:
            o_ref[0, :, pl.ds(h * wo, HEAD_DIM)] = x.astype(o_ref.dtype)
            o_ref[0, :, pl.ds(h * wo + HEAD_DIM, HEAD_DIM)] = shared


def _prep(x, *, mode, heads, xcol, x2=None, x2col=None, gain=None, tables=None, scale=1.0, ts=256):
    b, t, _ = x.shape
    ts = _tile(t, ts)
    wo = HEAD_DIM if mode == "head" else 2 * HEAD_DIM
    in_specs = [pl.BlockSpec((1, ts, x.shape[2]), lambda bb, i: (bb, i, 0))]
    args = [x]
    if mode == "mla_k":
        in_specs.append(pl.BlockSpec((1, ts, x2.shape[2]), lambda bb, i: (bb, i, 0)))
        args.append(x2)
    if gain is not None:
        in_specs.append(pl.BlockSpec((1, HEAD_DIM), lambda bb, i: (0, 0)))
        args.append(gain)
    if tables is not None:
        tspec = pl.BlockSpec((ts, LANES), lambda bb, i: (i, 0))
        in_specs += [tspec, tspec]
        args += list(tables)
    return pl.pallas_call(
        functools.partial(_prep_kernel, mode=mode, rope=tables is not None, norm=gain is not None,
                          scale=scale, heads=heads, xcol=xcol, x2col=x2col),
        grid=(b, t // ts),
        in_specs=in_specs,
        out_specs=pl.BlockSpec((1, ts, heads * wo), lambda bb, i: (bb, i, 0)),
        out_shape=jax.ShapeDtypeStruct((b, t, heads * wo), BF16),
        compiler_params=_cparams(("parallel", "parallel")),
        name="head_prep_" + mode,
    )(*args)


def _flash_kernel(*refs, nk, scale, has_lat, n_sub):
    if has_lat:
        q_ref, kc_ref, vc_ref, kl_ref, vl_ref, o_ref, m_ref, acc_ref, va_ref = refs
    else:
        q_ref, kc_ref, vc_ref, o_ref, m_ref, acc_ref, va_ref = refs
    kk = pl.program_id(3)
    tq = q_ref.shape[1]
    dv = o_ref.shape[2]
    tr = tq // n_sub

    def step(k, v):
        tk = k.shape[0]
        va_ref[pl.ds(0, tk), pl.ds(0, dv)] = v
        va = va_ref[pl.ds(0, tk), :]

        def scores(r):
            q = q_ref[0, pl.ds(r * tr, tr), :]
            if scale != 1.0:
                q = (q.astype(F32) * scale).astype(BF16)
            return lax.dot_general(q, k, (((1,), (1,)), ((), ())), preferred_element_type=F32)

        s_next = scores(0)
        for r in range(n_sub):
            rows = pl.ds(r * tr, tr)
            s = s_next
            if r + 1 < n_sub:
                s_next = scores(r + 1)
            m_prev = m_ref[rows, :]
            m_new = jnp.maximum(m_prev, jnp.max(s, axis=-1, keepdims=True))
            alpha = jnp.exp2(m_prev - m_new)
            p = jnp.exp2((s - _lane_tile(m_new, tk // LANES)).astype(BF16))
            acc_ref[rows, :] = (_lane_tile(alpha, 2 * dv // LANES) * acc_ref[rows, :]
                                + jnp.dot(p, va, preferred_element_type=F32))
            m_ref[rows, :] = m_new

    @pl.when(kk == 0)
    def _():
        m_ref[...] = jnp.full_like(m_ref, NEG_INF)
        acc_ref[...] = jnp.zeros_like(acc_ref)
        va_ref[:, pl.ds(dv, dv)] = jnp.ones((va_ref.shape[0], dv), va_ref.dtype)
        step(kc_ref[0], vc_ref[0])

    if has_lat:
        @pl.when(kk > 0)
        def _():
            step(kl_ref[0], vl_ref[0])

    @pl.when(kk == nk - 1)
    def _():
        o_ref[0] = (acc_ref[:, pl.ds(0, dv)] / acc_ref[:, pl.ds(dv, dv)]).astype(o_ref.dtype)


def _flash(q, kc, vc, kl=None, vl=None, *, heads, dq, dv, qcol, kcol, vcol, scale=1.0,
           tq=8192, tk=2048, tr=512):
    b, s, _ = q.shape
    tc = kc.shape[1]
    tq = _tile(s, tq)
    has_lat = kl is not None
    if has_lat:
        tk = _tile(kl.shape[1], tk)
        nk = 1 + kl.shape[1] // tk
    else:
        nk = 1
    in_specs = [
        pl.BlockSpec((1, tq, dq), lambda bb, h, i, kk: (bb, i, qcol(h))),
        pl.BlockSpec((1, tc, dq), lambda bb, h, i, kk: (bb, 0, kcol(h))),
        pl.BlockSpec((1, tc, dv), lambda bb, h, i, kk: (bb, 0, vcol(h))),
    ]
    args = [q, kc, vc]
    if has_lat:
        in_specs += [
            pl.BlockSpec((1, tk, dq), lambda bb, h, i, kk: (bb, jnp.maximum(kk - 1, 0), kcol(h))),
            pl.BlockSpec((1, tk, dv), lambda bb, h, i, kk: (bb, jnp.maximum(kk - 1, 0), vcol(h))),
        ]
        args += [kl, vl]
    return pl.pallas_call(
        functools.partial(_flash_kernel, nk=nk, scale=scale, has_lat=has_lat, n_sub=tq // _tile(tq, tr)),
        grid=(b, heads, s // tq, nk),
        in_specs=in_specs,
        out_specs=pl.BlockSpec((1, tq, dv), lambda bb, h, i, kk: (bb, i, h)),
        out_shape=jax.ShapeDtypeStruct((b, s, heads * dv), BF16),
        scratch_shapes=[pltpu.VMEM((tq, LANES), F32), pltpu.VMEM((tq, 2 * dv), F32),
                        pltpu.VMEM((max(tc, tk), 2 * dv), BF16)],
        compiler_params=_cparams(("parallel", "parallel", "parallel", "arbitrary")),
        name="flash_attention",
    )(*args)


def _na_kernel(q_ref, k_ref, v_ref, kc_ref, vc_ref, *rest, band, seq, scale, n_sub, blk):
    bias_refs, o_ref = rest[:-1], rest[-1]
    i = pl.program_id(2)
    nb = len(bias_refs)
    dv = o_ref.shape[2]
    kc = kc_ref[0]
    vc = jnp.concatenate([vc_ref[0], jnp.ones((kc.shape[0], dv), BF16)], axis=1)
    dn = (((1,), (1,)), ((), ()))
    tr = blk // n_sub
    for j in range(nb):
        start = jnp.clip((i * nb + j) * blk - (band - blk) // 2, 0, seq - band)
        start = pl.multiple_of(start, 256)
        kb = k_ref[0, pl.ds(start, band), :]
        vb = jnp.concatenate([v_ref[0, pl.ds(start, band), :], jnp.ones((band, dv), BF16)], axis=1)
        for r in range(n_sub):
            rows = pl.ds(j * blk + r * tr, tr)
            q = (q_ref[0, rows, :].astype(F32) * scale).astype(BF16)
            s_nb = (lax.dot_general(q, kb, dn, preferred_element_type=F32)
                    + bias_refs[j][0, 0, pl.ds(r * tr, tr), :])
            s_cx = lax.dot_general(q, kc, dn, preferred_element_type=F32)
            m = jnp.maximum(jnp.max(s_nb, axis=-1, keepdims=True), jnp.max(s_cx, axis=-1, keepdims=True))
            p_nb = jnp.exp2((s_nb - m).astype(BF16))
            p_cx = jnp.exp2((s_cx - m).astype(BF16))
            o = (jnp.dot(p_nb, vb, preferred_element_type=F32)
                 + jnp.dot(p_cx, vc, preferred_element_type=F32))
            o_ref[0, rows, :] = (o[:, :dv] / o[:, dv:]).astype(o_ref.dtype)


def _na_bias_tiles(rpb, rows):
    nr, wr, wc, w = NA_QROWS, NA_WIN_R, NA_WIN_C, GRID_W
    brows = nr + wr
    heads = rpb.shape[0]
    lpad = w - wc
    vp = jnp.pad(rpb.astype(F32) * LOG2_E, ((0, 0), (0, 0), (lpad, lpad)), constant_values=NEG_INF)
    toep = jnp.stack([vp[:, :, w - 1 - c: 2 * w - 1 - c] for c in range(w)], axis=2)
    col = jnp.arange(w, dtype=jnp.int32)
    c0 = jnp.clip(col - wc // 2, 0, w - wc)
    in_col = (col[None, :] >= c0[:, None]) & (col[None, :] < c0[:, None] + wc)
    toep = jnp.where(in_col[None, None], toep, NEG_INF)
    masked = jnp.full((heads, 1, w, w), NEG_INF, F32)
    tiles = []
    for r_first, b_first in ((0, 0), (nr, nr - wr // 2), (rows - nr, rows - brows)):
        per_row = []
        for a in range(nr):
            r = r_first + a
            r0 = min(max(r - wr // 2, 0), rows - wr)
            lo = r0 - b_first
            dr_lo = r0 - r + (wr - 1)
            blk = toep[:, dr_lo: dr_lo + wr]
            blk = jnp.concatenate([jnp.tile(masked, (1, lo, 1, 1)), blk,
                                   jnp.tile(masked, (1, brows - wr - lo, 1, 1))], axis=1)
            per_row.append(jnp.swapaxes(blk, 1, 2))
        tiles.append(jnp.stack(per_row, axis=1).reshape(heads, nr * w, brows * w))
    return jnp.stack(tiles, axis=1)


def _na_attention(z, zc, rpb, *, heads):
    b, s, _ = z.shape
    tcx = zc.shape[1]
    rows = s // GRID_W
    tq = NA_QROWS * GRID_W
    band = (NA_QROWS + NA_WIN_R) * GRID_W
    nblk = s // tq
    bias = _na_bias_tiles(rpb, rows)
    scale = HEAD_DIM ** -0.5 * LOG2_E

    def variant(i):
        return jnp.where(i == 0, 0, jnp.where(i == nblk - 1, 2, 1))

    nb = next(n for n in (4, 2, 1) if nblk % n == 0)
    bias_specs = [pl.BlockSpec((1, 1, tq, band), lambda bb, h, i, j=j: (h, variant(i * nb + j), 0, 0))
                  for j in range(nb)]
    return pl.pallas_call(
        functools.partial(_na_kernel, band=band, seq=s, scale=scale, n_sub=4, blk=tq),
        grid=(b, heads, nblk // nb),
        in_specs=[
            pl.BlockSpec((1, nb * tq, HEAD_DIM), lambda bb, h, i: (bb, i, h)),
            pl.BlockSpec((1, s, HEAD_DIM), lambda bb, h, i: (bb, 0, heads + h)),
            pl.BlockSpec((1, s, HEAD_DIM), lambda bb, h, i: (bb, 0, 2 * heads + h)),
            pl.BlockSpec((1, tcx, HEAD_DIM), lambda bb, h, i: (bb, 0, heads + h)),
            pl.BlockSpec((1, tcx, HEAD_DIM), lambda bb, h, i: (bb, 0, 2 * heads + h)),
        ] + bias_specs,
        out_specs=pl.BlockSpec((1, nb * tq, HEAD_DIM), lambda bb, h, i: (bb, i, h)),
        out_shape=jax.ShapeDtypeStruct((b, s, heads * HEAD_DIM), BF16),
        compiler_params=_cparams(("parallel", "parallel", "arbitrary")),
        name="neighbourhood_attention",
    )(z, z, z, zc, zc, *([bias] * nb))


def _ret_kernel(dec_ref, *refs, nc):
    ins, (of_ref, ob_ref, st_ref, din_ref) = refs[:14], refs[14:]
    dirs = (ins[:7] + (of_ref,), ins[7:] + (ob_ref,))
    h = pl.program_id(1)
    t = pl.program_id(2)
    half = RET_QK // 2
    lgs = [jnp.log1p(-jnp.exp2(jnp.full((1, 1), dec_ref[d, h], F32))) for d in (0, 1)]

    def rope(x, cos_ref, sin_ref):
        c, s = cos_ref[...], sin_ref[...]
        x1, x2 = x[:, :half], x[:, half:]
        return jnp.concatenate([x1 * c - x2 * s, x2 * c + x1 * s], axis=-1)

    def update(d, k, v):
        n = k.shape[0]
        pos = lax.broadcasted_iota(jnp.int32, (n, 1), 0).astype(F32)
        zeta = jnp.exp((n - 1.0 - pos if d == 0 else pos) * lgs[d])
        kz = (k * zeta).astype(BF16)
        kv = lax.dot_general(kz, v, (((0,), (0,)), ((), ())), preferred_element_type=F32)
        st_ref[d] = st_ref[d] * jnp.exp(n * lgs[d]) + kv

    @pl.when(t == 0)
    def _():
        st_ref[...] = jnp.zeros_like(st_ref)
        n = din_ref.shape[1]
        ri = lax.broadcasted_iota(jnp.int32, (n, n), 0)
        ci = lax.broadcasted_iota(jnp.int32, (n, n), 1)
        for d in (0, 1):
            dist = ri - ci if d == 0 else ci - ri
            din_ref[d] = jnp.where(dist >= 0, jnp.exp(jnp.maximum(dist, 0).astype(F32) * lgs[d]), 0.0)

    @pl.when(t < nc)
    def _():
        for d in (0, 1):
            kc_ref, vc_ref = dirs[d][3], dirs[d][4]
            update(d, kc_ref[0].astype(F32) * RET_QK ** -0.5, vc_ref[0])

    @pl.when(t >= nc)
    def _():
        for d in (0, 1):
            q_ref, k_ref, v_ref, _, _, cos_ref, sin_ref, o_ref = dirs[d]
            q = rope(q_ref[0].astype(F32), cos_ref, sin_ref)
            k = rope(k_ref[0].astype(F32) * RET_QK ** -0.5, cos_ref, sin_ref)
            v = v_ref[0]
            n = q.shape[0]
            pos = lax.broadcasted_iota(jnp.int32, (n, 1), 0).astype(F32)
            qb = q.astype(BF16)
            s = lax.dot_general(qb, k.astype(BF16), (((1,), (1,)), ((), ())),
                                preferred_element_type=F32) * din_ref[d]
            xi = jnp.exp((pos + 1.0 if d == 0 else n - pos) * lgs[d])
            o = (jnp.dot(s.astype(BF16), v, preferred_element_type=F32)
                 + jnp.dot(qb, st_ref[d].astype(BF16), preferred_element_type=F32) * xi)
            o_ref[0] = o.astype(o_ref.dtype)
            update(d, k, v)


def _retention(z, zc, decay_log2, cos, sin, *, heads):
    b, s, _ = z.shape
    tcx = zc.shape[1]
    chunk = _tile(s, RET_CHUNK)
    cchunk = _tile(tcx, RET_CHUNK)
    nc, nl = tcx // cchunk, s // chunk
    kcol0 = heads
    vcol0 = (2 * heads * RET_QK) // RET_V

    def cidx(d, t):
        c = jnp.minimum(t, nc - 1)
        return c if d == 0 else nc - 1 - c

    def lidx(d, t):
        c = jnp.maximum(t - nc, 0)
        return c if d == 0 else nl - 1 - c

    in_specs = [pl.BlockSpec(memory_space=pltpu.SMEM)]
    out_specs = []
    for d in (0, 1):
        in_specs += [
            pl.BlockSpec((1, chunk, RET_QK), lambda bb, h, t, d=d: (bb, lidx(d, t), h)),
            pl.BlockSpec((1, chunk, RET_QK), lambda bb, h, t, d=d: (bb, lidx(d, t), kcol0 + h)),
            pl.BlockSpec((1, chunk, RET_V), lambda bb, h, t, d=d: (bb, lidx(d, t), vcol0 + h)),
            pl.BlockSpec((1, cchunk, RET_QK), lambda bb, h, t, d=d: (bb, cidx(d, t), kcol0 + h)),
            pl.BlockSpec((1, cchunk, RET_V), lambda bb, h, t, d=d: (bb, cidx(d, t), vcol0 + h)),
            pl.BlockSpec((chunk, RET_QK // 2), lambda bb, h, t, d=d: (lidx(d, t), 0)),
            pl.BlockSpec((chunk, RET_QK // 2), lambda bb, h, t, d=d: (lidx(d, t), 0)),
        ]
        out_specs.append(pl.BlockSpec((1, chunk, RET_V), lambda bb, h, t, d=d: (bb, lidx(d, t), h)))
    out = jax.ShapeDtypeStruct((b, s, heads * RET_V), BF16)
    return pl.pallas_call(
        functools.partial(_ret_kernel, nc=nc),
        grid=(b, heads, nc + nl),
        in_specs=in_specs,
        out_specs=out_specs,
        out_shape=[out, out],
        scratch_shapes=[pltpu.VMEM((2, RET_QK, RET_V), F32), pltpu.VMEM((2, chunk, chunk), F32)],
        compiler_params=_cparams(("parallel", "parallel", "arbitrary")),
        name="retention_scan",
    )(decay_log2, *([z, z, z, zc, zc, cos, sin] * 2))


def _ret_gate_kernel(of_ref, ob_ref, g_ref, y_ref, *, heads):
    for h in range(heads):
        cols = pl.ds(h * RET_V, RET_V)
        o = of_ref[0, :, cols].astype(F32) + ob_ref[0, :, cols].astype(F32)
        y_ref[0, :, cols] = (_silu(g_ref[0, :, cols].astype(F32)) * _rms(o)).astype(y_ref.dtype)


def _ret_gate(o_fwd, o_bwd, z, *, heads, ts=256):
    b, s, width = o_fwd.shape
    ts = _tile(s, ts)
    gcol = (2 * heads * RET_QK + heads * RET_V) // width
    spec = pl.BlockSpec((1, ts, width), lambda bb, i: (bb, i, 0))
    return pl.pallas_call(
        functools.partial(_ret_gate_kernel, heads=heads),
        grid=(b, s // ts),
        in_specs=[spec, spec, pl.BlockSpec((1, ts, width), lambda bb, i: (bb, i, gcol))],
        out_specs=spec,
        out_shape=jax.ShapeDtypeStruct((b, s, width), BF16),
        compiler_params=_cparams(("parallel", "parallel")),
        name="retention_gate",
    )(o_fwd, o_bwd, z)


def _axial_angles(n_tokens, rot_dim):
    t = jnp.arange(n_tokens, dtype=jnp.int32)
    row = (t // GRID_W).astype(F32)
    col = (t % GRID_W).astype(F32)
    n_f = rot_dim // 4
    freqs = ROPE_BASE ** (-jnp.arange(n_f, dtype=F32) / n_f)
    return jnp.concatenate([row[:, None] * freqs, col[:, None] * freqs], axis=-1)


def _rope_slot_tables(ang):
    pad = jnp.zeros((ang.shape[0], 64 - ang.shape[1]), F32)
    cos, sin = jnp.cos(ang), jnp.sin(ang)
    return (jnp.concatenate([cos, pad, cos, pad], axis=-1),
            jnp.concatenate([-sin, pad, sin, pad], axis=-1))


def _spread_rope_cols(w):
    half = MLA_ROPE // 2
    z = jnp.zeros(w.shape[:-1] + (64 - half,), w.dtype)
    return jnp.concatenate([w[..., :half], z, w[..., half:], z], axis=-1)


def _mla_mixer(in_proj, b, s, tcx, heads, w_in, q_gain, kv_gain, w_uq, w_ukv):
    rq = q_gain.shape[-1]
    w_in = jnp.concatenate([w_in[:, :2 * rq], _spread_rope_cols(w_in[:, 2 * rq:])], axis=-1).astype(BF16)
    w_uq = w_uq.reshape(rq, heads, HEAD_DIM + MLA_ROPE)
    w_uq = jnp.concatenate([w_uq[..., :HEAD_DIM], _spread_rope_cols(w_uq[..., HEAD_DIM:])],
                           axis=-1).reshape(rq, heads * 2 * HEAD_DIM).astype(BF16)
    w_ukv = w_ukv.astype(BF16)
    tables = _rope_slot_tables(_axial_angles(s, MLA_ROPE))
    scale = (HEAD_DIM + MLA_ROPE) ** -0.5 * LOG2_E
    zl, zc = in_proj(w_in, F32)
    ropecol = 2 * rq // LANES

    def qkv(z, n_tok, tabs):
        q = _proj(z, q_gain[None], w_uq, kblock=0, out_dtype=BF16).reshape(b, n_tok, -1)
        kv = _proj(z, kv_gain[None], w_ukv, kblock=1, out_dtype=BF16).reshape(b, n_tok, -1)
        q = _prep(q, mode="mla_q", heads=heads, xcol=lambda hh: hh, tables=tabs, scale=scale)
        k = _prep(kv, mode="mla_k", heads=heads, xcol=lambda hh: 2 * hh,
                  x2=z.reshape(b, n_tok, -1), x2col=ropecol, tables=tabs)
        return q, k, kv

    ql, kl, kvl = qkv(zl, s, tables)
    qc, kc, kvc = qkv(zc, tcx, None)
    cols = dict(heads=heads, dq=2 * HEAD_DIM, dv=HEAD_DIM, qcol=lambda hh: hh, kcol=lambda hh: hh,
                vcol=lambda hh: 2 * hh + 1)
    return _flash(ql, kc, kvc, kl, kvl, **cols), _flash(qc, kc, kvc, **cols)


def _gqa_mixer(in_proj, b, s, tcx, heads, w_in, q_gain, k_gain):
    hk = (w_in.shape[-1] // HEAD_DIM - heads) // 2
    grp = heads // hk
    tables = _rope_slot_tables(_axial_angles(s, HEAD_DIM))
    zl, zc = in_proj(w_in.astype(BF16), BF16)
    zl = zl.reshape(b, s, -1)
    zc = zc.reshape(b, tcx, -1)
    scale = HEAD_DIM ** -0.5 * LOG2_E
    qg, kg = q_gain[None], k_gain[None]
    ql = _prep(zl, mode="head", heads=heads, xcol=lambda hh: hh, gain=qg, tables=tables, scale=scale)
    kl = _prep(zl, mode="head", heads=hk, xcol=lambda hh: heads + hh, gain=kg, tables=tables)
    qc = _prep(zc, mode="head", heads=heads, xcol=lambda hh: hh, gain=qg, scale=scale)
    kc = _prep(zc, mode="head", heads=hk, xcol=lambda hh: heads + hh, gain=kg)
    cols = dict(heads=heads, dq=HEAD_DIM, dv=HEAD_DIM, qcol=lambda hh: hh,
                kcol=lambda hh: hh // grp, vcol=lambda hh: heads + hk + hh // grp)
    return _flash(ql, kc, zc, kl, zl, **cols), _flash(qc, kc, zc, **cols)


def _na_mixer(in_proj, b, s, tcx, heads, w_in, rpb):
    zl, zc = in_proj(w_in.astype(BF16), BF16)
    zl = zl.reshape(b, s, -1)
    zc = zc.reshape(b, tcx, -1)
    y = _na_attention(zl, zc, rpb, heads=heads)
    yc = _flash(zc, zc, zc, heads=heads, dq=HEAD_DIM, dv=HEAD_DIM, qcol=lambda hh: hh,
                kcol=lambda hh: heads + hh, vcol=lambda hh: 2 * heads + hh,
                scale=HEAD_DIM ** -0.5 * LOG2_E)
    return y, yc


def _ret_mixer(in_proj, b, s, tcx, w_in, decay_log2):
    rh = decay_log2.shape[-1]
    pos = jnp.arange(s, dtype=F32)
    freqs = ROPE_BASE ** (-jnp.arange(RET_QK // 2, dtype=F32) / (RET_QK // 2))
    ang = pos[:, None] * freqs
    zl, zc = in_proj(w_in.astype(BF16), BF16)
    zl = zl.reshape(b, s, -1)
    zc = zc.reshape(b, tcx, -1)
    o_fwd, o_bwd = _retention(zl, zc, decay_log2.astype(F32), jnp.cos(ang), jnp.sin(ang), heads=rh)
    return _ret_gate(o_fwd, o_bwd, zl, heads=rh)


def kernel(x, c, ctx, c_ctx, mod_w, mod_b, norm_g, ffn_w_gate, ffn_w_up, ffn_w_down,
           mla_w_in, mla_q_gain, mla_kv_gain, mla_w_uq, mla_w_ukv, mla_w_o,
           gqa_w_in, gqa_q_gain, gqa_k_gain, gqa_w_o,
           na_w_in, na_rpb, na_w_o,
           ret_w_in, ret_decay_log2, ret_w_o):
    b, s, d = x.shape
    tcx = ctx.shape[1]
    depth = mod_w.shape[0]
    assert depth == 4 and s % (NA_QROWS * GRID_W) == 0 and s // GRID_W >= NA_QROWS + NA_WIN_R
    heads = d // HEAD_DIM

    cc = jnp.concatenate([c, c_ctx[None], jnp.zeros((8 - b - 1, d), F32)], axis=0)
    mods = _mod_vectors(cc, mod_w, mod_b).reshape(depth, 8, N_MOD, d)

    h = x.reshape(b * s, d)
    hc = ctx.reshape(b * tcx, d)
    w_gate, w_up, w_down = ffn_w_gate.astype(BF16), ffn_w_up.astype(BF16), ffn_w_down.astype(BF16)

    for i in range(depth):
        kind = i % 4
        last = i == depth - 1
        ml = [mods[i, :b, j][:, None, :] for j in range(N_MOD)]
        mc = [mods[i, b:b + 1, j][:, None, :] for j in range(N_MOD)]
        gn = [norm_g[i, j][None, :] for j in range(6)]

        def ffn(hh, m, f):
            return _ffn(hh, m[3 * f], m[3 * f + 1], m[3 * f + 2], gn[2 * f], gn[2 * f + 1],
                        w_gate, w_up, w_down, (i, f // 2))

        h = ffn(h, ml, 0)
        hc = ffn(hc, mc, 0)

        def in_proj(w, out_dtype):
            zl = _proj(h, gn[2], w, shift=ml[3], scale=ml[4], out_dtype=out_dtype)
            zc = _proj(hc, gn[2], w, shift=mc[3], scale=mc[4], out_dtype=out_dtype)
            return zl, zc

        if kind == 0:
            y, yc = _mla_mixer(in_proj, b, s, tcx, heads, mla_w_in[0], mla_q_gain[0], mla_kv_gain[0],
                               mla_w_uq[0], mla_w_ukv[0])
            w_o = mla_w_o[0]
        elif kind == 1:
            y, yc = _gqa_mixer(in_proj, b, s, tcx, heads, gqa_w_in[0], gqa_q_gain[0], gqa_k_gain[0])
            w_o = gqa_w_o[0]
        elif kind == 2:
            y, yc = _na_mixer(in_proj, b, s, tcx, heads, na_w_in[0], na_rpb[0])
            w_o = na_w_o[0]
        else:
            y, yc = _ret_mixer(in_proj, b, s, tcx, ret_w_in[0], ret_decay_log2[0]), None
            w_o = ret_w_o[0]

        w_o = w_o.astype(BF16)
        h = _out_proj(y.reshape(b * s, -1), w_o, gn[3], ml[5], h)
        h = ffn(h, ml, 2)
        if not last:
            hc = _out_proj(yc.reshape(b * tcx, -1), w_o, gn[3], mc[5], hc)
            hc = ffn(hc, mc, 2)
    return h.reshape(b, s, d)
```
